```python
import jax, jax.numpy as jnp
from jax import lax
import numpy as np

D_MODEL = 1024
BATCH = 8
SEQ = 2048
DEPTH = 1
DEC_BATCH = 128
DEC_SEQ = 8
PAST_LEN = 16384
PAGE_SIZE = 128

MIX_DIM = D_MODEL
CONV_DIM = MIX_DIM // 2
CONV_HEADS = 8
CONV_W = 3
POOL_DIM = MIX_DIM - CONV_DIM
POOL_WINDOWS = (2, 4, 8, 16)
POOL_GROUPS = len(POOL_WINDOWS)
POOL_GDIM = POOL_DIM // POOL_GROUPS
POOL_HIST = max(POOL_WINDOWS) - 1
IN_DIM = 3 * CONV_DIM + POOL_DIM
D_FF = 2816
EPS = 1e-6

kernel_name = "hybrid_conv_pool_macaron_decode_step"


def rmsnorm(x, g):
    xf = x.astype(jnp.float32)
    inv = lax.rsqrt(jnp.mean(xf * xf, axis=-1, keepdims=True) + EPS)
    return (xf * inv).astype(x.dtype) * g


def swiglu(h, w_gate, w_up, w_down):
    return (jax.nn.silu(h @ w_gate) * (h @ w_up)) @ w_down


def short_conv(u, hist, conv_w):
    T = u.shape[1]
    ext = jnp.concatenate([hist, u], axis=1)
    out = ext[:, 0:T] * conv_w[0]
    for k in range(1, CONV_W):
        out = out + ext[:, k:k + T] * conv_w[k]
    return out, ext[:, -(CONV_W - 1):]


def multi_scale_pool(p, hist, p0, pool_w, pool_scale):
    Bn, T, _ = p.shape
    ext = jnp.concatenate([hist, p], axis=1)
    extf = ext.astype(jnp.float32)
    cs = jnp.concatenate([jnp.zeros((Bn, 1, POOL_DIM), jnp.float32), jnp.cumsum(extf, axis=1)], axis=1)
    pos = p0 + jnp.arange(T, dtype=jnp.int32)
    outs = []
    for g, w in enumerate(POOL_WINDOWS):
        sl = slice(g * POOL_GDIM, (g + 1) * POOL_GDIM)
        wsum = cs[:, POOL_HIST + 1:POOL_HIST + 1 + T, sl] - cs[:, POOL_HIST + 1 - w:POOL_HIST + 1 - w + T, sl]
        cnt = jnp.minimum(pos + 1, w).astype(jnp.float32)[None, :, None]
        d = (wsum / cnt - extf[:, POOL_HIST:, sl]).astype(p.dtype)
        outs.append(jnp.einsum('btc,cd->btd', d, pool_w[g]))
    y = jnp.concatenate(outs, axis=-1) * pool_scale
    return y, ext[:, -POOL_HIST:]


def layer_step(x, conv_hist, pool_hist, p0,
               norm_ffn1, ffn1_gate, ffn1_up, ffn1_down,
               norm_mix, w_in, conv_w, pool_w, pool_scale, w_out,
               norm_ffn2, ffn2_gate, ffn2_up, ffn2_down, norm_final):
    x = x + 0.5 * swiglu(rmsnorm(x, norm_ffn1), ffn1_gate, ffn1_up, ffn1_down)
    h = rmsnorm(x, norm_mix)
    proj = h @ w_in
    gb = proj[..., 0:CONV_DIM]
    gc = proj[..., CONV_DIM:2 * CONV_DIM]
    u = proj[..., 2 * CONV_DIM:3 * CONV_DIM]
    p = proj[..., 3 * CONV_DIM:]
    conv_out, new_conv = short_conv(gc * u, conv_hist, conv_w)
    y_conv = gb * conv_out
    y_pool, new_pool = multi_scale_pool(p, pool_hist, p0, pool_w, pool_scale)
    x = x + jnp.concatenate([y_conv, y_pool], axis=-1) @ w_out
    x = x + 0.5 * swiglu(rmsnorm(x, norm_ffn2), ffn2_gate, ffn2_up, ffn2_down)
    return rmsnorm(x, norm_final), new_conv, new_pool


def setup_inputs(seed: int = 0) -> dict:
    key = jax.random.key(seed)
    ks = jax.random.split(key, 24)
    f = jnp.float32
    nrm = lambda k, shape, s: jax.random.normal(k, shape, f) * s
    return {
        "x_prompt": nrm(ks[0], (BATCH, SEQ, D_MODEL), 1.0),
        "x_sample": nrm(ks[1], (DEC_BATCH, DEC_SEQ, D_MODEL), 1.0),
        "state_conv": nrm(ks[2], (DEC_BATCH, CONV_W - 1, CONV_DIM), 0.5),
        "state_pool": nrm(ks[3], (DEC_BATCH, POOL_HIST, POOL_DIM), 1.0),
        "norm_ffn1": 1.0 + nrm(ks[4], (D_MODEL,), 0.05),
        "ffn1_gate": nrm(ks[5], (D_MODEL, D_FF), D_MODEL ** -0.5),
        "ffn1_up": nrm(ks[6], (D_MODEL, D_FF), D_MODEL ** -0.5),
        "ffn1_down": nrm(ks[7], (D_FF, D_MODEL), D_FF ** -0.5),
        "norm_mix": 1.0 + nrm(ks[8], (D_MODEL,), 0.05),
        "w_in": nrm(ks[9], (D_MODEL, IN_DIM), D_MODEL ** -0.5),
        "conv_w": nrm(ks[10], (CONV_W, CONV_DIM), CONV_W ** -0.5),
        "pool_w": nrm(ks[11], (POOL_GROUPS, POOL_GDIM, POOL_GDIM), POOL_GDIM ** -0.5),
        "pool_scale": 1.0 + nrm(ks[12], (POOL_DIM,), 0.05),
        "w_out": nrm(ks[13], (MIX_DIM, D_MODEL), MIX_DIM ** -0.5),
        "norm_ffn2": 1.0 + nrm(ks[14], (D_MODEL,), 0.05),
        "ffn2_gate": nrm(ks[15], (D_MODEL, D_FF), D_MODEL ** -0.5),
        "ffn2_up": nrm(ks[16], (D_MODEL, D_FF), D_MODEL ** -0.5),
        "ffn2_down": nrm(ks[17], (D_FF, D_MODEL), D_FF ** -0.5),
        "norm_final": 1.0 + nrm(ks[18], (D_MODEL,), 0.05),
    }


def reference(x_prompt, x_sample, state_conv, state_pool,
              norm_ffn1, ffn1_gate, ffn1_up, ffn1_down,
              norm_mix, w_in, conv_w, pool_w, pool_scale, w_out,
              norm_ffn2, ffn2_gate, ffn2_up, ffn2_down, norm_final):
    weights = (norm_ffn1, ffn1_gate, ffn1_up, ffn1_down,
               norm_mix, w_in, conv_w, pool_w, pool_scale, w_out,
               norm_ffn2, ffn2_gate, ffn2_up, ffn2_down, norm_final)
    hp, cp, pp = x_prompt, jnp.zeros((x_prompt.shape[0], CONV_W - 1, CONV_DIM), x_prompt.dtype), \
        jnp.zeros((x_prompt.shape[0], POOL_HIST, POOL_DIM), x_prompt.dtype)
    hs, cs_, ps = x_sample, state_conv.astype(x_sample.dtype), state_pool.astype(x_sample.dtype)
    for _ in range(DEPTH):
        hp, cp, pp = layer_step(hp, cp, pp, 0, *weights)
        hs, cs_, ps = layer_step(hs, cs_, ps, PAST_LEN, *weights)
    return (hp, hs, cp, pp, cs_, ps)
```

```python
import functools

import jax
import jax.numpy as jnp
from jax.experimental import pallas as pl
from jax.experimental.pallas import tpu as pltpu

EPS = 1e-6
PAST_LEN = 16384
CONV_W = 3
POOL_WINDOWS = (2, 4, 8, 16)
POOL_HIST = max(POOL_WINDOWS) - 1
SUBLANES = 8
POOL_PAD = 16
CONV_PAD = 8
MXU_COLS = 256
V7X_VMEM_BYTES = 64 * 1024 * 1024
FFN_TOKENS = 512
MIX_TOKENS = 512
MIX_SEQS = 64

BF16 = jnp.bfloat16
F32 = jnp.float32


def _rms(x, gain):
    inv = jax.lax.rsqrt(jnp.mean(x * x, axis=-1, keepdims=True) + EPS)
    return x * inv * gain


def _dot(a, b):
    return jnp.dot(a, b, preferred_element_type=F32)


def _ffn_kernel(x_ref, gain_ref, wg_ref, wu_ref, wd_ref, gain_out_ref, o_ref, *, final_norm):
    x = x_ref[...]
    h = _rms(x, gain_ref[...]).astype(BF16)
    d_ff = wg_ref.shape[1]
    acc = jnp.zeros(x.shape, F32)
    for c0 in range(0, d_ff, MXU_COLS):
        g = _dot(h, wg_ref[:, c0:c0 + MXU_COLS])
        u = _dot(h, wu_ref[:, c0:c0 + MXU_COLS])
        a = (g * jax.nn.sigmoid(g) * u).astype(BF16)
        acc = acc + _dot(a, wd_ref[c0:c0 + MXU_COLS, :])
    y = x + 0.5 * acc
    if final_norm:
        y = _rms(y, gain_out_ref[...])
    o_ref[...] = y


def _resident(shape):
    return pl.BlockSpec(shape, lambda *_: (0,) * len(shape), pipeline_mode=pl.Buffered(1))


def _ffn(x, gain, wg, wu, wd, gain_out, *, final_norm, name):
    n, d = x.shape
    d_ff = wg.shape[1]
    tm = FFN_TOKENS
    assert n % tm == 0 and d_ff % MXU_COLS == 0
    weight_bytes = 3 * d * d_ff * 2
    tile_bytes = tm * d * 4
    vmem = weight_bytes + 12 * tile_bytes
    assert vmem < V7X_VMEM_BYTES
    return pl.pallas_call(
        functools.partial(_ffn_kernel, final_norm=final_norm),
        grid=(n // tm,),
        in_specs=[
            pl.BlockSpec((tm, d), lambda i: (i, 0)),
            _resident((1, d)),
            _resident((d, d_ff)),
            _resident((d, d_ff)),
            _resident((d_ff, d)),
            _resident((1, d)),
        ],
        out_specs=pl.BlockSpec((tm, d), lambda i: (i, 0)),
        out_shape=jax.ShapeDtypeStruct((n, d), F32),
        compiler_params=pltpu.CompilerParams(
            dimension_semantics=("arbitrary",), vmem_limit_bytes=vmem),
        name=name,
    )(x, gain, wg, wu, wd, gain_out)


def _mix_rows(x, proj, conv_ext, pool_ext, first_pos, convw_ref, poolw_ref, pscale_ref, wout_ref):
    c = conv_ext.shape[-1]
    t = x.shape[-2]
    lead = (slice(None),) * (x.ndim - 2)
    gb = proj[..., 0:c]
    gc = proj[..., c:2 * c]
    u = proj[..., 2 * c:3 * c]
    p = proj[..., 3 * c:4 * c]

    conv_ext[lead + (slice(CONV_PAD, CONV_PAD + t),)] = gc * u
    conv = None
    for k in range(CONV_W):
        r0 = CONV_PAD - (CONV_W - 1) + k
        term = conv_ext[lead + (slice(r0, r0 + t),)] * convw_ref[k:k + 1, :]
        conv = term if conv is None else conv + term
    y_conv = gb * conv

    pool_ext[lead + (slice(POOL_PAD, POOL_PAD + t),)] = p
    gdim = c // len(POOL_WINDOWS)
    pos = first_pos + jax.lax.broadcasted_iota(jnp.int32, (t, 1), 0)
    y_pool = []
    for g, w in enumerate(POOL_WINDOWS):
        lanes = slice(g * gdim, (g + 1) * gdim)
        wsum = None
        for j in range(w):
            term = pool_ext[lead + (slice(POOL_PAD - j, POOL_PAD - j + t), lanes)]
            wsum = term if wsum is None else wsum + term
        cnt = jnp.minimum(pos + 1, w).astype(F32)
        dlt = (wsum / cnt - p[..., lanes]).astype(BF16)
        y = _dot(dlt.reshape(-1, gdim), poolw_ref[g]).reshape(dlt.shape)
        y_pool.append(y * pscale_ref[:, lanes])
    mix = jnp.concatenate([y_conv] + y_pool, axis=-1).astype(BF16)
    out = _dot(mix.reshape(-1, 2 * c), wout_ref[...])
    return x + out.reshape(x.shape)


def _mixer_prompt_kernel(x_ref, gain_ref, win_ref, convw_ref, poolw_ref, pscale_ref, wout_ref,
                         o_ref, conv_state_ref, pool_state_ref, conv_ext, pool_ext):
    ti = pl.program_id(1)
    t = x_ref.shape[1]

    @pl.when(ti == 0)
    def _():
        conv_ext[0:CONV_PAD, :] = jnp.zeros((CONV_PAD, conv_ext.shape[1]), F32)
        pool_ext[0:POOL_PAD, :] = jnp.zeros((POOL_PAD, pool_ext.shape[1]), F32)

    @pl.when(ti > 0)
    def _():
        conv_ext[0:CONV_PAD, :] = conv_ext[t:t + CONV_PAD, :]
        pool_ext[0:POOL_PAD, :] = pool_ext[t:t + POOL_PAD, :]

    x = x_ref[0]
    h = _rms(x, gain_ref[...]).astype(BF16)
    proj = _dot(h, win_ref[...])
    o_ref[0] = _mix_rows(x, proj, conv_ext, pool_ext, ti * t,
                         convw_ref, poolw_ref, pscale_ref, wout_ref)
    conv_state_ref[0] = conv_ext[CONV_PAD + t - (CONV_W - 1):CONV_PAD + t, :]
    pool_state_ref[0] = pool_ext[POOL_PAD + t - POOL_HIST:POOL_PAD + t, :]


def _mixer_sample_kernel(x_ref, conv_hist_ref, pool_hist_ref, gain_ref, win_ref, convw_ref,
                         poolw_ref, pscale_ref, wout_ref,
                         o_ref, conv_state_ref, pool_state_ref, conv_ext, pool_ext, *, first_pos):
    bs, t, d = x_ref.shape
    conv_ext[:, CONV_PAD - (CONV_W - 1):CONV_PAD, :] = conv_hist_ref[...]
    pool_ext[:, POOL_PAD - POOL_HIST:POOL_PAD, :] = pool_hist_ref[...]
    x = x_ref[...]
    h = _rms(x, gain_ref[...]).astype(BF16)
    proj = _dot(h.reshape(bs * t, d), win_ref[...]).reshape(bs, t, win_ref.shape[1])
    o_ref[...] = _mix_rows(x, proj, conv_ext, pool_ext, first_pos,
                           convw_ref, poolw_ref, pscale_ref, wout_ref)
    conv_state_ref[...] = conv_ext[:, CONV_PAD + t - (CONV_W - 1):CONV_PAD + t, :]
    pool_state_ref[...] = pool_ext[:, POOL_PAD + t - POOL_HIST:POOL_PAD + t, :]


def _mixer_weight_specs(d, c, in_dim, groups):
    return [
        _resident((1, d)),
        _resident((d, in_dim)),
        _resident((CONV_W, c)),
        _resident((groups, c // groups, c // groups)),
        _resident((1, c)),
        _resident((2 * c, d)),
    ]


def _mixer_vmem(d, in_dim, c, tokens, ext_rows):
    weight_bytes = (d * in_dim + 2 * c * d + c * c) * 2
    tile_bytes = tokens * d * 4
    return weight_bytes + 12 * tile_bytes + 4 * ext_rows * c * 4


def _mixer_prompt(x, gain, win, convw, poolw, pscale, wout):
    b, s, d = x.shape
    in_dim = win.shape[1]
    c = in_dim // 4
    t = MIX_TOKENS
    assert s % t == 0
    vmem = _mixer_vmem(d, in_dim, c, t, t + POOL_PAD)
    assert vmem < V7X_VMEM_BYTES
    return pl.pallas_call(
        _mixer_prompt_kernel,
        grid=(b, s // t),
        in_specs=[pl.BlockSpec((1, t, d), lambda i, j: (i, j, 0))]
        + _mixer_weight_specs(d, c, in_dim, len(POOL_WINDOWS)),
        out_specs=[
            pl.BlockSpec((1, t, d), lambda i, j: (i, j, 0)),
            pl.BlockSpec((1, CONV_W - 1, c), lambda i, j: (i, 0, 0)),
            pl.BlockSpec((1, POOL_HIST, c), lambda i, j: (i, 0, 0)),
        ],
        out_shape=[
            jax.ShapeDtypeStruct((b, s, d), F32),
            jax.ShapeDtypeStruct((b, CONV_W - 1, c), F32),
            jax.ShapeDtypeStruct((b, POOL_HIST, c), F32),
        ],
        scratch_shapes=[
            pltpu.VMEM((CONV_PAD + t, c), F32),
            pltpu.VMEM((POOL_PAD + t, c), F32),
        ],
        compiler_params=pltpu.CompilerParams(
            dimension_semantics=("arbitrary", "arbitrary"), vmem_limit_bytes=vmem),
        name="mixer_prompt",
    )(x, gain, win, convw, poolw, pscale, wout)


def _mixer_sample(x, conv_hist, pool_hist, gain, win, convw, poolw, pscale, wout, *, first_pos):
    b, t, d = x.shape
    in_dim = win.shape[1]
    c = in_dim // 4
    bs = MIX_SEQS
    assert b % bs == 0 and t == SUBLANES
    vmem = _mixer_vmem(d, in_dim, c, bs * t, bs * (t + POOL_PAD))
    assert vmem < V7X_VMEM_BYTES
    seq_block = lambda rows, width: pl.BlockSpec((bs, rows, width), lambda i: (i, 0, 0))
    return pl.pallas_call(
        functools.partial(_mixer_sample_kernel, first_pos=first_pos),
        grid=(b // bs,),
        in_specs=[seq_block(t, d), seq_block(CONV_W - 1, c), seq_block(POOL_HIST, c)]
        + _mixer_weight_specs(d, c, in_dim, len(POOL_WINDOWS)),
        out_specs=[seq_block(t, d), seq_block(CONV_W - 1, c), seq_block(POOL_HIST, c)],
        out_shape=[
            jax.ShapeDtypeStruct((b, t, d), F32),
            jax.ShapeDtypeStruct((b, CONV_W - 1, c), F32),
            jax.ShapeDtypeStruct((b, POOL_HIST, c), F32),
        ],
        scratch_shapes=[
            pltpu.VMEM((bs, CONV_PAD + t, c), F32),
            pltpu.VMEM((bs, POOL_PAD + t, c), F32),
        ],
        compiler_params=pltpu.CompilerParams(
            dimension_semantics=("arbitrary",), vmem_limit_bytes=vmem),
        name="mixer_sample",
    )(x, conv_hist, pool_hist, gain, win, convw, poolw, pscale, wout)


def kernel(x_prompt, x_sample, state_conv, state_pool, norm_ffn1, ffn1_gate, ffn1_up, ffn1_down,
           norm_mix, w_in, conv_w, pool_w, pool_scale, w_out, norm_ffn2, ffn2_gate, ffn2_up,
           ffn2_down, norm_final):
    d = x_prompt.shape[-1]
    row = lambda v: v.reshape(1, -1)
    ffn1 = (row(norm_ffn1), ffn1_gate.astype(BF16), ffn1_up.astype(BF16), ffn1_down.astype(BF16))
    ffn2 = (row(norm_ffn2), ffn2_gate.astype(BF16), ffn2_up.astype(BF16), ffn2_down.astype(BF16))
    mix = (row(norm_mix), w_in.astype(BF16), conv_w, pool_w.astype(BF16), row(pool_scale),
           w_out.astype(BF16))
    gain_final = row(norm_final)

    def layer(x, mixer, tag):
        shape = x.shape
        x = _ffn(x.reshape(-1, d), *ffn1, gain_final, final_norm=False, name="ffn1_" + tag)
        x, conv_state, pool_state = mixer(x.reshape(shape))
        x = _ffn(x.reshape(-1, d), *ffn2, gain_final, final_norm=True, name="ffn2_" + tag)
        return x.reshape(shape), conv_state, pool_state

    yp, cp, pp = layer(x_prompt, lambda x: _mixer_prompt(x, *mix), "prompt")
    ys, cs, ps = layer(
        x_sample,
        lambda x: _mixer_sample(x, state_conv, state_pool, *mix, first_pos=PAST_LEN),
        "sample")
    return (yp, ys, cp, pp, cs, ps)
```

```python
import functools

import jax
import jax.numpy as jnp
from jax.experimental import pallas as pl
from jax.experimental.pallas import tpu as pltpu

EPS = 1e-6
PAST_LEN = 16384
CONV_W = 3
POOL_WINDOWS = (2, 4, 8, 16)
POOL_HIST = max(POOL_WINDOWS) - 1
SUBLANES = 8
POOL_PAD = 16
CONV_PAD = 8
MXU_COLS = 256
V7X_VMEM_BYTES = 64 * 1024 * 1024
FFN_TOKENS = 512
MIX_TOKENS = 512
MIX_SEQS = 64

BF16 = jnp.bfloat16
F32 = jnp.float32


def _rms(x, gain):
    inv = jax.lax.rsqrt(jnp.mean(x * x, axis=-1, keepdims=True) + EPS)
    return x * inv * gain


def _dot(a, b):
    return jax.lax.dot_general(a, b, (((a.ndim - 1,), (0,)), ((), ())), preferred_element_type=F32)


def _ffn_kernel(x_ref, gain_ref, wg_ref, wu_ref, wd_ref, gain_out_ref, o_ref, *, final_norm):
    x = x_ref[...]
    h = _rms(x, gain_ref[...]).astype(BF16)
    d_ff = wg_ref.shape[1]
    acc = jnp.zeros(x.shape, F32)
    for c0 in range(0, d_ff, MXU_COLS):
        g = _dot(h, wg_ref[:, c0:c0 + MXU_COLS])
        u = _dot(h, wu_ref[:, c0:c0 + MXU_COLS])
        a = (g * jax.nn.sigmoid(g) * u).astype(BF16)
        acc = acc + _dot(a, wd_ref[c0:c0 + MXU_COLS, :])
    y = x + 0.5 * acc
    if final_norm:
        y = _rms(y, gain_out_ref[...])
    o_ref[...] = y


def _resident(shape):
    return pl.BlockSpec(shape, lambda *_: (0,) * len(shape), pipeline_mode=pl.Buffered(1))


def _ffn(x, gain, wg, wu, wd, gain_out, *, final_norm, name):
    n, d = x.shape
    d_ff = wg.shape[1]
    tm = FFN_TOKENS
    assert n % tm == 0 and d_ff % MXU_COLS == 0
    weight_bytes = 3 * d * d_ff * 4
    tile_bytes = tm * d * 4
    vmem = weight_bytes + 12 * tile_bytes
    assert vmem < V7X_VMEM_BYTES
    return pl.pallas_call(
        functools.partial(_ffn_kernel, final_norm=final_norm),
        grid=(n // tm,),
        in_specs=[
            pl.BlockSpec((tm, d), lambda i: (i, 0)),
            _resident((1, d)),
            _resident((d, d_ff)),
            _resident((d, d_ff)),
            _resident((d_ff, d)),
            _resident((1, d)),
        ],
        out_specs=pl.BlockSpec((tm, d), lambda i: (i, 0)),
        out_shape=jax.ShapeDtypeStruct((n, d), F32),
        compiler_params=pltpu.CompilerParams(
            dimension_semantics=("arbitrary",), vmem_limit_bytes=vmem),
        name=name,
    )(x, gain, wg, wu, wd, gain_out)


def _mix_rows(x, proj, conv_ext, pool_ext, first_pos, convw_ref, poolw_ref, pscale_ref, wout_ref):
    c = conv_ext.shape[-1]
    t = x.shape[-2]
    lead = (slice(None),) * (x.ndim - 2)
    gb = proj[..., 0:c]
    gc = proj[..., c:2 * c]
    u = proj[..., 2 * c:3 * c]
    p = proj[..., 3 * c:4 * c]

    conv_ext[lead + (slice(CONV_PAD, CONV_PAD + t),)] = gc * u
    conv = None
    for k in range(CONV_W):
        r0 = CONV_PAD - (CONV_W - 1) + k
        term = conv_ext[lead + (slice(r0, r0 + t),)] * convw_ref[k:k + 1, :]
        conv = term if conv is None else conv + term
    y_conv = gb * conv

    pool_ext[lead + (slice(POOL_PAD, POOL_PAD + t),)] = p
    gdim = c // len(POOL_WINDOWS)
    pos = first_pos + jax.lax.broadcasted_iota(jnp.int32, (t, 1), 0)
    y_pool = []
    for g, w in enumerate(POOL_WINDOWS):
        lanes = slice(g * gdim, (g + 1) * gdim)
        wsum = None
        for j in range(w):
            term = pool_ext[lead + (slice(POOL_PAD - j, POOL_PAD - j + t), lanes)]
            wsum = term if wsum is None else wsum + term
        cnt = jnp.minimum(pos + 1, w).astype(F32)
        dlt = (wsum / cnt - p[..., lanes]).astype(BF16)
        y = _dot(dlt.reshape(-1, gdim), poolw_ref[g]).reshape(dlt.shape)
        y_pool.append(y * pscale_ref[:, lanes])
    mix = jnp.concatenate([y_conv] + y_pool, axis=-1).astype(BF16)
    out = _dot(mix.reshape(-1, 2 * c), wout_ref[...])
    return x + out.reshape(x.shape)


def _mixer_prompt_kernel(x_ref, gain_ref, win_ref, convw_ref, poolw_ref, pscale_ref, wout_ref,
                         o_ref, conv_state_ref, pool_state_ref, conv_ext, pool_ext):
    ti = pl.program_id(1)
    t = x_ref.shape[1]

    @pl.when(ti == 0)
    def _():
        conv_ext[0:CONV_PAD, :] = jnp.zeros((CONV_PAD, conv_ext.shape[1]), F32)
        pool_ext[0:POOL_PAD, :] = jnp.zeros((POOL_PAD, pool_ext.shape[1]), F32)

    @pl.when(ti > 0)
    def _():
        conv_ext[0:CONV_PAD, :] = conv_ext[t:t + CONV_PAD, :]
        pool_ext[0:POOL_PAD, :] = pool_ext[t:t + POOL_PAD, :]

    x = x_ref[0]
    h = _rms(x, gain_ref[...]).astype(BF16)
    proj = _dot(h, win_ref[...])
    o_ref[0] = _mix_rows(x, proj, conv_ext, pool_ext, ti * t,
                         convw_ref, poolw_ref, pscale_ref, wout_ref)
    conv_state_ref[0] = conv_ext[CONV_PAD + t - (CONV_W - 1):CONV_PAD + t, :]
    pool_state_ref[0] = pool_ext[POOL_PAD + t - POOL_HIST:POOL_PAD + t, :]


def _mixer_sample_kernel(x_ref, conv_hist_ref, pool_hist_ref, gain_ref, win_ref, convw_ref,
                         poolw_ref, pscale_ref, wout_ref,
                         o_ref, conv_state_ref, pool_state_ref, conv_ext, pool_ext, *, first_pos):
    bs, t, d = x_ref.shape
    conv_ext[:, CONV_PAD - (CONV_W - 1):CONV_PAD, :] = conv_hist_ref[...]
    pool_ext[:, POOL_PAD - POOL_HIST:POOL_PAD, :] = pool_hist_ref[...]
    x = x_ref[...]
    h = _rms(x, gain_ref[...]).astype(BF16)
    proj = _dot(h.reshape(bs * t, d), win_ref[...]).reshape(bs, t, win_ref.shape[1])
    o_ref[...] = _mix_rows(x, proj, conv_ext, pool_ext, first_pos,
                           convw_ref, poolw_ref, pscale_ref, wout_ref)
    conv_state_ref[...] = conv_ext[:, CONV_PAD + t - (CONV_W - 1):CONV_PAD + t, :]
    pool_state_ref[...] = pool_ext[:, POOL_PAD + t - POOL_HIST:POOL_PAD + t, :]


def _mixer_weight_specs(d, c, in_dim, groups):
    return [
        _resident((1, d)),
        _resident((d, in_dim)),
        _resident((CONV_W, c)),
        _resident((groups, c // groups, c // groups)),
        _resident((1, c)),
        _resident((2 * c, d)),
    ]


def _mixer_vmem(d, in_dim, c, tokens, ext_rows):
    weight_bytes = (d * in_dim + 2 * c * d + c * c) * 4
    tile_bytes = tokens * d * 4
    return weight_bytes + 12 * tile_bytes + 4 * ext_rows * c * 4


def _mixer_prompt(x, gain, win, convw, poolw, pscale, wout):
    b, s, d = x.shape
    in_dim = win.shape[1]
    c = in_dim // 4
    t = MIX_TOKENS
    assert s % t == 0
    vmem = _mixer_vmem(d, in_dim, c, t, t + POOL_PAD)
    assert vmem < V7X_VMEM_BYTES
    return pl.pallas_call(
        _mixer_prompt_kernel,
        grid=(b, s // t),
        in_specs=[pl.BlockSpec((1, t, d), lambda i, j: (i, j, 0))]
        + _mixer_weight_specs(d, c, in_dim, len(POOL_WINDOWS)),
        out_specs=[
            pl.BlockSpec((1, t, d), lambda i, j: (i, j, 0)),
            pl.BlockSpec((1, CONV_W - 1, c), lambda i, j: (i, 0, 0)),
            pl.BlockSpec((1, POOL_HIST, c), lambda i, j: (i, 0, 0)),
        ],
        out_shape=[
            jax.ShapeDtypeStruct((b, s, d), F32),
            jax.ShapeDtypeStruct((b, CONV_W - 1, c), F32),
            jax.ShapeDtypeStruct((b, POOL_HIST, c), F32),
        ],
        scratch_shapes=[
            pltpu.VMEM((CONV_PAD + t, c), F32),
            pltpu.VMEM((POOL_PAD + t, c), F32),
        ],
        compiler_params=pltpu.CompilerParams(
            dimension_semantics=("arbitrary", "arbitrary"), vmem_limit_bytes=vmem),
        name="mixer_prompt",
    )(x, gain, win, convw, poolw, pscale, wout)


def _mixer_sample(x, conv_hist, pool_hist, gain, win, convw, poolw, pscale, wout, *, first_pos):
    b, t, d = x.shape
    in_dim = win.shape[1]
    c = in_dim // 4
    bs = MIX_SEQS
    assert b % bs == 0 and t == SUBLANES
    vmem = _mixer_vmem(d, in_dim, c, bs * t, bs * (t + POOL_PAD))
    assert vmem < V7X_VMEM_BYTES
    seq_block = lambda rows, width: pl.BlockSpec((bs, rows, width), lambda i: (i, 0, 0))
    return pl.pallas_call(
        functools.partial(_mixer_sample_kernel, first_pos=first_pos),
        grid=(b // bs,),
        in_specs=[seq_block(t, d), seq_block(CONV_W - 1, c), seq_block(POOL_HIST, c)]
        + _mixer_weight_specs(d, c, in_dim, len(POOL_WINDOWS)),
        out_specs=[seq_block(t, d), seq_block(CONV_W - 1, c), seq_block(POOL_HIST, c)],
        out_shape=[
            jax.ShapeDtypeStruct((b, t, d), F32),
            jax.ShapeDtypeStruct((b, CONV_W - 1, c), F32),
            jax.ShapeDtypeStruct((b, POOL_HIST, c), F32),
        ],
        scratch_shapes=[
            pltpu.VMEM((bs, CONV_PAD + t, c), F32),
            pltpu.VMEM((bs, POOL_PAD + t, c), F32),
        ],
        compiler_params=pltpu.CompilerParams(
            dimension_semantics=("arbitrary",), vmem_limit_bytes=vmem),
        name="mixer_sample",
    )(x, conv_hist, pool_hist, gain, win, convw, poolw, pscale, wout)


def kernel(x_prompt, x_sample, state_conv, state_pool, norm_ffn1, ffn1_gate, ffn1_up, ffn1_down,
           norm_mix, w_in, conv_w, pool_w, pool_scale, w_out, norm_ffn2, ffn2_gate, ffn2_up,
           ffn2_down, norm_final):
    d = x_prompt.shape[-1]
    row = lambda v: v.reshape(1, -1)
    ffn1 = (row(norm_ffn1), ffn1_gate, ffn1_up, ffn1_down)
    ffn2 = (row(norm_ffn2), ffn2_gate, ffn2_up, ffn2_down)
    mix = (row(norm_mix), w_in, conv_w, pool_w, row(pool_scale), w_out)
    gain_final = row(norm_final)

    def layer(x, mixer, tag):
        shape = x.shape
        x = _ffn(x.reshape(-1, d), *ffn1, gain_final, final_norm=False, name="ffn1_" + tag)
        x, conv_state, pool_state = mixer(x.reshape(shape))
        x = _ffn(x.reshape(-1, d), *ffn2, gain_final, final_norm=True, name="ffn2_" + tag)
        return x.reshape(shape), conv_state, pool_state

    yp, cp, pp = layer(x_prompt, lambda x: _mixer_prompt(x, *mix), "prompt")
    ys, cs, ps = layer(
        x_sample,
        lambda x: _mixer_sample(x, state_conv, state_pool, *mix, first_pos=PAST_LEN),
        "sample")
    return (yp, ys, cp, pp, cs, ps)
```

```python
import functools

import jax
import jax.numpy as jnp
from jax.experimental import pallas as pl
from jax.experimental.pallas import tpu as pltpu

EPS = 1e-6
PAST_LEN = 16384
CONV_W = 3
POOL_WINDOWS = (2, 4, 8, 16)
POOL_HIST = max(POOL_WINDOWS) - 1
SUBLANES = 8
POOL_PAD = 24
CONV_PAD = 8
MXU_COLS = 256
V7X_VMEM_BYTES = 64 * 1024 * 1024
TOKENS = 512

BF16 = jnp.bfloat16
F32 = jnp.float32


def _rms(x, gain):
    inv = jax.lax.rsqrt(jnp.mean(x * x, axis=-1, keepdims=True) + EPS)
    return x * inv * gain


def _dot(a, b):
    return jax.lax.dot_general(a, b, (((a.ndim - 1,), (0,)), ((), ())), preferred_element_type=F32)


def _resident(shape):
    return pl.BlockSpec(shape, lambda *_: (0,) * len(shape), pipeline_mode=pl.Buffered(1))


def _group_specs(n_prompt_steps, steps_per_seq, seqs_per_step, t, d):
    def prompt_map(i):
        j = jnp.minimum(i, n_prompt_steps - 1)
        return (j // steps_per_seq, j % steps_per_seq, 0)

    def sample_map(i):
        return (jnp.maximum(i - n_prompt_steps, 0), 0, 0)

    return (pl.BlockSpec((1, t, d), prompt_map),
            pl.BlockSpec((seqs_per_step, SUBLANES, d), sample_map))


def _ffn_kernel(*refs, n_prompt_steps, split_in, split_out, final_norm):
    n_in = 2 if split_in else 1
    x_refs, (gain_ref, wg_ref, wu_ref, wd_ref, gain_out_ref), o_refs = (
        refs[:n_in], refs[n_in:n_in + 5], refs[n_in + 5:])
    on_prompt = pl.program_id(0) < n_prompt_steps
    if split_in:
        xa_ref, xb_ref = x_refs
        x = jnp.where(on_prompt, xa_ref[0], xb_ref[...].reshape(xa_ref.shape[1:]))
    else:
        x = x_refs[0][...]
    h = _rms(x, gain_ref[...]).astype(BF16)
    d_ff = wg_ref.shape[1]
    acc = jnp.zeros(x.shape, F32)
    for c0 in range(0, d_ff, MXU_COLS):
        g = _dot(h, wg_ref[:, c0:c0 + MXU_COLS])
        u = _dot(h, wu_ref[:, c0:c0 + MXU_COLS])
        a = (g * jax.nn.sigmoid(g) * u).astype(BF16)
        acc = acc + _dot(a, wd_ref[c0:c0 + MXU_COLS, :])
    y = x + 0.5 * acc
    if final_norm:
        y = _rms(y, gain_out_ref[...])
    if split_out:
        oa_ref, ob_ref = o_refs

        @pl.when(on_prompt)
        def _():
            oa_ref[0] = y

        @pl.when(jnp.logical_not(on_prompt))
        def _():
            ob_ref[...] = y.reshape(ob_ref.shape)
    else:
        o_refs[0][...] = y


def _ffn(x, gain, wg, wu, wd, gain_out, *, group_shapes, split_in, split_out, final_norm, name):
    (b, s, d), (nb, tb, _) = group_shapes
    d_ff = wg.shape[1]
    t = TOKENS
    assert s % t == 0 and tb == SUBLANES and (nb * tb) % t == 0 and d_ff % MXU_COLS == 0
    n_prompt_steps = b * s // t
    n_steps = n_prompt_steps + nb * tb // t
    pair_specs = _group_specs(n_prompt_steps, s // t, t // tb, t, d)
    flat_spec = pl.BlockSpec((t, d), lambda i: (i, 0))
    pair_shapes = [jax.ShapeDtypeStruct(shape, F32) for shape in group_shapes]
    flat_shape = jax.ShapeDtypeStruct((n_steps * t, d), F32)
    vmem = 3 * d * d_ff * wg.dtype.itemsize + 12 * t * d * 4
    assert vmem < V7X_VMEM_BYTES
    return pl.pallas_call(
        functools.partial(_ffn_kernel, n_prompt_steps=n_prompt_steps, split_in=split_in,
                          split_out=split_out, final_norm=final_norm),
        grid=(n_steps,),
        in_specs=[*(pair_specs if split_in else (flat_spec,)),
                  _resident((1, d)), _resident((d, d_ff)), _resident((d, d_ff)),
                  _resident((d_ff, d)), _resident((1, d))],
        out_specs=list(pair_specs) if split_out else flat_spec,
        out_shape=pair_shapes if split_out else flat_shape,
        compiler_params=pltpu.CompilerParams(
            dimension_semantics=("arbitrary",), vmem_limit_bytes=vmem),
        name=name,
    )(*(x if split_in else (x,)), gain, wg, wu, wd, gain_out)


def _rows(lead, start, size, lanes=slice(None)):
    return lead + (slice(start, start + size), lanes)


def _mix_rows(x, proj, conv_ext, pool_ext, pool_lvl, first_pos,
              convw_ref, poolw_ref, pscale_ref, wout_ref):
    c = conv_ext.shape[-1]
    t = x.shape[-2]
    lead = (slice(None),) * (x.ndim - 2)
    gb = proj[..., 0:c]
    gc = proj[..., c:2 * c]
    u = proj[..., 2 * c:3 * c]
    p = proj[..., 3 * c:4 * c]

    conv_ext[_rows(lead, CONV_PAD, t)] = gc * u
    conv = None
    for k in range(CONV_W):
        term = conv_ext[_rows(lead, CONV_PAD - (CONV_W - 1) + k, t)] * convw_ref[k:k + 1, :]
        conv = term if conv is None else conv + term
    y_conv = gb * conv

    pool_ext[_rows(lead, POOL_PAD, t)] = p
    gdim = c // len(POOL_WINDOWS)
    wsums = []
    if pool_lvl is None:
        for g, w in enumerate(POOL_WINDOWS):
            lanes = slice(g * gdim, (g + 1) * gdim)
            wsum = pool_ext[_rows(lead, POOL_PAD, t, lanes)]
            for j in range(1, w):
                wsum = wsum + pool_ext[_rows(lead, POOL_PAD - j, t, lanes)]
            wsums.append(wsum)
    else:
        span = POOL_PAD - SUBLANES + t
        src = pool_ext
        for g, w in enumerate(POOL_WINDOWS):
            lanes = slice(g * gdim, c)
            shift = w // 2
            if g < len(pool_lvl):
                dst = pool_lvl[g]
                dst[_rows(lead, SUBLANES, span, lanes)] = (
                    src[_rows(lead, SUBLANES, span, lanes)]
                    + src[_rows(lead, SUBLANES - shift, span, lanes)])
                wsums.append(dst[_rows(lead, POOL_PAD, t, slice(g * gdim, (g + 1) * gdim))])
                src = dst
            else:
                wsums.append(src[_rows(lead, POOL_PAD, t, lanes)]
                             + src[_rows(lead, POOL_PAD - shift, t, lanes)])

    pos = first_pos + jax.lax.broadcasted_iota(jnp.int32, (t, 1), 0)
    y_pool = []
    for g, w in enumerate(POOL_WINDOWS):
        lanes = slice(g * gdim, (g + 1) * gdim)
        cnt = jnp.minimum(pos + 1, w).astype(F32)
        dlt = (wsums[g] / cnt - p[..., lanes]).astype(BF16)
        y = _dot(dlt.reshape(-1, gdim), poolw_ref[g]).reshape(dlt.shape)
        y_pool.append(y * pscale_ref[:, lanes])
    mix = jnp.concatenate([y_conv] + y_pool, axis=-1).astype(BF16)
    out = _dot(mix.reshape(-1, 2 * c), wout_ref[...])
    return x + out.reshape(x.shape)


def _state_rows(ext, pad, t, n):
    lead = (slice(None),) * (len(ext.shape) - 2)
    return ext[_rows(lead, pad + t - n, n)]


def _mixer_kernel(x_ref, conv_hist_ref, pool_hist_ref, gain_ref, win_ref, convw_ref,
                  poolw_ref, pscale_ref, wout_ref,
                  o_ref, conv_a_ref, pool_a_ref, conv_b_ref, pool_b_ref,
                  conv_ext_a, pool_ext_a, lvl1_a, lvl2_a, lvl3_a, conv_ext_b, pool_ext_b,
                  *, n_prompt_steps, steps_per_seq):
    i = pl.program_id(0)
    t, d = x_ref.shape
    weights = (convw_ref, poolw_ref, pscale_ref, wout_ref)

    @pl.when(i == 0)
    def _():
        for ref in (pool_ext_a, lvl1_a, lvl2_a):
            ref[0:POOL_PAD - POOL_HIST, :] = jnp.zeros((POOL_PAD - POOL_HIST, ref.shape[-1]), F32)

    @pl.when(i < n_prompt_steps)
    def _():
        ti = i % steps_per_seq

        @pl.when(ti == 0)
        def _():
            conv_ext_a[0:CONV_PAD, :] = jnp.zeros((CONV_PAD, conv_ext_a.shape[1]), F32)
            pool_ext_a[SUBLANES:POOL_PAD, :] = jnp.zeros(
                (POOL_PAD - SUBLANES, pool_ext_a.shape[1]), F32)

        @pl.when(ti > 0)
        def _():
            conv_ext_a[0:CONV_PAD, :] = conv_ext_a[t:t + CONV_PAD, :]
            pool_ext_a[SUBLANES:POOL_PAD, :] = pool_ext_a[t + SUBLANES:t + POOL_PAD, :]

        x = x_ref[...]
        h = _rms(x, gain_ref[...]).astype(BF16)
        proj = _dot(h, win_ref[...])
        o_ref[...] = _mix_rows(x, proj, conv_ext_a, pool_ext_a, (lvl1_a, lvl2_a, lvl3_a),
                               ti * t, *weights)
        conv_a_ref[0] = _state_rows(conv_ext_a, CONV_PAD, t, CONV_W - 1)
        pool_a_ref[0] = _state_rows(pool_ext_a, POOL_PAD, t, POOL_HIST)

    @pl.when(i >= n_prompt_steps)
    def _():
        bs, tb = conv_ext_b.shape[0], conv_ext_b.shape[1] - CONV_PAD
        conv_ext_b[:, CONV_PAD - (CONV_W - 1):CONV_PAD, :] = conv_hist_ref[...]
        pool_ext_b[:, POOL_PAD - POOL_HIST:POOL_PAD, :] = pool_hist_ref[...]
        x = x_ref[...]
        h = _rms(x, gain_ref[...]).astype(BF16)
        proj = _dot(h, win_ref[...])
        y = _mix_rows(x.reshape(bs, tb, d), proj.reshape(bs, tb, win_ref.shape[1]),
                      conv_ext_b, pool_ext_b, None, PAST_LEN, *weights)
        o_ref[...] = y.reshape(t, d)
        conv_b_ref[...] = _state_rows(conv_ext_b, CONV_PAD, tb, CONV_W - 1)
        pool_b_ref[...] = _state_rows(pool_ext_b, POOL_PAD, tb, POOL_HIST)


def _mixer(x, conv_hist, pool_hist, gain, win, convw, poolw, pscale, wout, *, group_shapes):
    (b, s, d), (nb, tb, _) = group_shapes
    in_dim = win.shape[1]
    c = in_dim // 4
    groups = len(POOL_WINDOWS)
    t = TOKENS
    bs = t // tb
    assert s % t == 0 and tb == SUBLANES and nb % bs == 0
    n_prompt_steps = b * s // t
    n_steps = n_prompt_steps + nb // bs
    assert x.shape == (n_steps * t, d)
    flat_spec = pl.BlockSpec((t, d), lambda i: (i, 0))

    def prompt_state(rows):
        return pl.BlockSpec(
            (1, rows, c), lambda i: (jnp.minimum(i, n_prompt_steps - 1) // (s // t), 0, 0))

    def sample_state(rows):
        return pl.BlockSpec((bs, rows, c), lambda i: (jnp.maximum(i - n_prompt_steps, 0), 0, 0))

    def state_shape(n, rows):
        return jax.ShapeDtypeStruct((n, rows, c), F32)

    def ext_scratch(lead, rows):
        return pltpu.VMEM(lead + (rows, c), F32)

    scratch = ([ext_scratch((), CONV_PAD + t)] + [ext_scratch((), POOL_PAD + t)] * 4
               + [ext_scratch((bs,), CONV_PAD + tb), ext_scratch((bs,), POOL_PAD + tb)])
    weight_bytes = (d * in_dim + 2 * c * d + c * c) * win.dtype.itemsize
    scratch_bytes = (5 * (POOL_PAD + t) + 2 * bs * (POOL_PAD + tb)) * c * 4
    vmem = weight_bytes + 12 * t * d * 4 + scratch_bytes
    assert vmem < V7X_VMEM_BYTES
    return pl.pallas_call(
        functools.partial(_mixer_kernel, n_prompt_steps=n_prompt_steps, steps_per_seq=s // t),
        grid=(n_steps,),
        in_specs=[flat_spec, sample_state(CONV_W - 1), sample_state(POOL_HIST),
                  _resident((1, d)), _resident((d, in_dim)), _resident((CONV_W, c)),
                  _resident((groups, c // groups, c // groups)), _resident((1, c)),
                  _resident((2 * c, d))],
        out_specs=[flat_spec, prompt_state(CONV_W - 1), prompt_state(POOL_HIST),
                   sample_state(CONV_W - 1), sample_state(POOL_HIST)],
        out_shape=[jax.ShapeDtypeStruct(x.shape, F32),
                   state_shape(b, CONV_W - 1), state_shape(b, POOL_HIST),
                   state_shape(nb, CONV_W - 1), state_shape(nb, POOL_HIST)],
        scratch_shapes=scratch,
        compiler_params=pltpu.CompilerParams(
            dimension_semantics=("arbitrary",), vmem_limit_bytes=vmem),
        name="mixer",
    )(x, conv_hist, pool_hist, gain, win, convw, poolw, pscale, wout)


def kernel(x_prompt, x_sample, state_conv, state_pool, norm_ffn1, ffn1_gate, ffn1_up, ffn1_down,
           norm_mix, w_in, conv_w, pool_w, pool_scale, w_out, norm_ffn2, ffn2_gate, ffn2_up,
           ffn2_down, norm_final):
    row = lambda v: v.reshape(1, -1)
    gain_final = row(norm_final)
    group_shapes = (x_prompt.shape, x_sample.shape)
    x = _ffn((x_prompt, x_sample), row(norm_ffn1), ffn1_gate, ffn1_up, ffn1_down, gain_final,
             group_shapes=group_shapes, split_in=True, split_out=False, final_norm=False,
             name="ffn1")
    x, cp, pp, cs, ps = _mixer(x, state_conv, state_pool, row(norm_mix), w_in, conv_w, pool_w,
                               row(pool_scale), w_out, group_shapes=group_shapes)
    yp, ys = _ffn(x, row(norm_ffn2), ffn2_gate, ffn2_up, ffn2_down, gain_final,
                  group_shapes=group_shapes, split_in=False, split_out=True, final_norm=True,
                  name="ffn2")
    return (yp, ys, cp, pp, cs, ps)
```

```python
import functools

import jax
import jax.numpy as jnp
from jax.experimental import pallas as pl
from jax.experimental.pallas import tpu as pltpu

EPS = 1e-6
PAST_LEN = 16384
CONV_W = 3
POOL_WINDOWS = (2, 4, 8, 16)
POOL_HIST = max(POOL_WINDOWS) - 1
SUBLANES = 8
POOL_PAD = 24
CONV_PAD = 8
MXU_COLS = 256
V7X_VMEM_BYTES = 64 * 1024 * 1024
FFN_TOKENS = 1024
MIX_TOKENS = 512

BF16 = jnp.bfloat16
F32 = jnp.float32


def _rms(x, gain):
    inv = jax.lax.rsqrt(jnp.mean(x * x, axis=-1, keepdims=True) + EPS)
    return x * inv * gain


def _dot(a, b):
    return jax.lax.dot_general(a, b, (((a.ndim - 1,), (0,)), ((), ())), preferred_element_type=F32)


def _resident(shape):
    return pl.BlockSpec(shape, lambda *_: (0,) * len(shape), pipeline_mode=pl.Buffered(1))


def _group_specs(n_prompt_steps, steps_per_seq, seqs_per_step, t, d):
    def prompt_map(i):
        j = jnp.minimum(i, n_prompt_steps - 1)
        return (j // steps_per_seq, j % steps_per_seq, 0)

    def sample_map(i):
        return (jnp.maximum(i - n_prompt_steps, 0), 0, 0)

    return (pl.BlockSpec((1, t, d), prompt_map),
            pl.BlockSpec((seqs_per_step, SUBLANES, d), sample_map))


def _ffn_kernel(*refs, n_prompt_steps, split_in, split_out, final_norm):
    n_in = 2 if split_in else 1
    x_refs, (gain_ref, wg_ref, wu_ref, wd_ref, gain_out_ref), o_refs = (
        refs[:n_in], refs[n_in:n_in + 5], refs[n_in + 5:])
    on_prompt = pl.program_id(0) < n_prompt_steps
    if split_in:
        xa_ref, xb_ref = x_refs
        x = jnp.where(on_prompt, xa_ref[0], xb_ref[...].reshape(xa_ref.shape[1:]))
    else:
        x = x_refs[0][...]
    h = _rms(x, gain_ref[...]).astype(BF16)
    d_ff = wg_ref.shape[1]
    acc = jnp.zeros(x.shape, F32)
    for c0 in range(0, d_ff, MXU_COLS):
        g = _dot(h, wg_ref[:, c0:c0 + MXU_COLS])
        u = _dot(h, wu_ref[:, c0:c0 + MXU_COLS])
        a = (g * jax.nn.sigmoid(g) * u).astype(BF16)
        acc = acc + _dot(a, wd_ref[c0:c0 + MXU_COLS, :])
    y = x + 0.5 * acc
    if final_norm:
        y = _rms(y, gain_out_ref[...])
    if split_out:
        oa_ref, ob_ref = o_refs

        @pl.when(on_prompt)
        def _():
            oa_ref[0] = y

        @pl.when(jnp.logical_not(on_prompt))
        def _():
            ob_ref[...] = y.reshape(ob_ref.shape)
    else:
        o_refs[0][...] = y


def _ffn(x, gain, wg, wu, wd, gain_out, *, group_shapes, split_in, split_out, final_norm, name):
    (b, s, d), (nb, tb, _) = group_shapes
    d_ff = wg.shape[1]
    t = FFN_TOKENS
    assert s % t == 0 and tb == SUBLANES and (nb * tb) % t == 0 and d_ff % MXU_COLS == 0
    n_prompt_steps = b * s // t
    n_steps = n_prompt_steps + nb * tb // t
    pair_specs = _group_specs(n_prompt_steps, s // t, t // tb, t, d)
    flat_spec = pl.BlockSpec((t, d), lambda i: (i, 0))
    pair_shapes = [jax.ShapeDtypeStruct(shape, F32) for shape in group_shapes]
    flat_shape = jax.ShapeDtypeStruct((n_steps * t, d), F32)
    vmem = 3 * d * d_ff * wg.dtype.itemsize + 10 * t * d * 4
    assert vmem < V7X_VMEM_BYTES
    return pl.pallas_call(
        functools.partial(_ffn_kernel, n_prompt_steps=n_prompt_steps, split_in=split_in,
                          split_out=split_out, final_norm=final_norm),
        grid=(n_steps,),
        in_specs=[*(pair_specs if split_in else (flat_spec,)),
                  _resident((1, d)), _resident((d, d_ff)), _resident((d, d_ff)),
                  _resident((d_ff, d)), _resident((1, d))],
        out_specs=list(pair_specs) if split_out else flat_spec,
        out_shape=pair_shapes if split_out else flat_shape,
        compiler_params=pltpu.CompilerParams(
            dimension_semantics=("arbitrary",), vmem_limit_bytes=vmem),
        name=name,
    )(*(x if split_in else (x,)), gain, wg, wu, wd, gain_out)


def _rows(lead, start, size, lanes=slice(None)):
    return lead + (slice(start, start + size), lanes)


def _mix_rows(x, proj, conv_ext, pool_ext, pool_lvl, first_pos,
              convw_ref, poolw_ref, pscale_ref, wout_ref):
    c = conv_ext.shape[-1]
    t = x.shape[-2]
    lead = (slice(None),) * (x.ndim - 2)
    gb = proj[..., 0:c]
    gc = proj[..., c:2 * c]
    u = proj[..., 2 * c:3 * c]
    p = proj[..., 3 * c:4 * c]

    conv_ext[_rows(lead, CONV_PAD, t)] = gc * u
    conv = None
    for k in range(CONV_W):
        term = conv_ext[_rows(lead, CONV_PAD - (CONV_W - 1) + k, t)] * convw_ref[k:k + 1, :]
        conv = term if conv is None else conv + term
    y_conv = gb * conv

    pool_ext[_rows(lead, POOL_PAD, t)] = p
    gdim = c // len(POOL_WINDOWS)
    wsums = []
    if pool_lvl is None:
        for g, w in enumerate(POOL_WINDOWS):
            lanes = slice(g * gdim, (g + 1) * gdim)
            wsum = pool_ext[_rows(lead, POOL_PAD, t, lanes)]
            for j in range(1, w):
                wsum = wsum + pool_ext[_rows(lead, POOL_PAD - j, t, lanes)]
            wsums.append(wsum)
    else:
        span = POOL_PAD - SUBLANES + t
        src = pool_ext
        for g, w in enumerate(POOL_WINDOWS):
            lanes = slice(g * gdim, c)
            shift = w // 2
            if g < len(pool_lvl):
                dst = pool_lvl[g]
                dst[_rows(lead, SUBLANES, span, lanes)] = (
                    src[_rows(lead, SUBLANES, span, lanes)]
                    + src[_rows(lead, SUBLANES - shift, span, lanes)])
                wsums.append(dst[_rows(lead, POOL_PAD, t, slice(g * gdim, (g + 1) * gdim))])
                src = dst
            else:
                wsums.append(src[_rows(lead, POOL_PAD, t, lanes)]
                             + src[_rows(lead, POOL_PAD - shift, t, lanes)])

    pos = first_pos + jax.lax.broadcasted_iota(jnp.int32, (t, 1), 0)
    y_pool = []
    for g, w in enumerate(POOL_WINDOWS):
        lanes = slice(g * gdim, (g + 1) * gdim)
        cnt = jnp.minimum(pos + 1, w).astype(F32)
        dlt = (wsums[g] / cnt - p[..., lanes]).astype(BF16)
        y = _dot(dlt.reshape(-1, gdim), poolw_ref[g]).reshape(dlt.shape)
        y_pool.append(y * pscale_ref[:, lanes])
    mix = jnp.concatenate([y_conv] + y_pool, axis=-1).astype(BF16)
    out = _dot(mix.reshape(-1, 2 * c), wout_ref[...])
    return x + out.reshape(x.shape)


def _state_rows(ext, pad, t, n):
    lead = (slice(None),) * (len(ext.shape) - 2)
    return ext[_rows(lead, pad + t - n, n)]


def _mixer_kernel(x_ref, conv_hist_ref, pool_hist_ref, gain_ref, win_ref, convw_ref,
                  poolw_ref, pscale_ref, wout_ref,
                  o_ref, conv_a_ref, pool_a_ref, conv_b_ref, pool_b_ref,
                  conv_ext_a, pool_ext_a, lvl1_a, lvl2_a, lvl3_a, conv_ext_b, pool_ext_b,
                  *, n_prompt_steps, steps_per_seq):
    i = pl.program_id(0)
    t, d = x_ref.shape
    weights = (convw_ref, poolw_ref, pscale_ref, wout_ref)

    @pl.when(i == 0)
    def _():
        for ref in (pool_ext_a, lvl1_a, lvl2_a):
            ref[0:POOL_PAD - POOL_HIST, :] = jnp.zeros((POOL_PAD - POOL_HIST, ref.shape[-1]), F32)

    @pl.when(i < n_prompt_steps)
    def _():
        ti = i % steps_per_seq

        @pl.when(ti == 0)
        def _():
            conv_ext_a[0:CONV_PAD, :] = jnp.zeros((CONV_PAD, conv_ext_a.shape[1]), F32)
            pool_ext_a[SUBLANES:POOL_PAD, :] = jnp.zeros(
                (POOL_PAD - SUBLANES, pool_ext_a.shape[1]), F32)

        @pl.when(ti > 0)
        def _():
            conv_ext_a[0:CONV_PAD, :] = conv_ext_a[t:t + CONV_PAD, :]
            pool_ext_a[SUBLANES:POOL_PAD, :] = pool_ext_a[t + SUBLANES:t + POOL_PAD, :]

        x = x_ref[...]
        h = _rms(x, gain_ref[...]).astype(BF16)
        proj = _dot(h, win_ref[...])
        o_ref[...] = _mix_rows(x, proj, conv_ext_a, pool_ext_a, (lvl1_a, lvl2_a, lvl3_a),
                               ti * t, *weights)
        conv_a_ref[0] = _state_rows(conv_ext_a, CONV_PAD, t, CONV_W - 1)
        pool_a_ref[0] = _state_rows(pool_ext_a, POOL_PAD, t, POOL_HIST)

    @pl.when(i >= n_prompt_steps)
    def _():
        bs, tb = conv_ext_b.shape[0], conv_ext_b.shape[1] - CONV_PAD
        conv_ext_b[:, CONV_PAD - (CONV_W - 1):CONV_PAD, :] = conv_hist_ref[...]
        pool_ext_b[:, POOL_PAD - POOL_HIST:POOL_PAD, :] = pool_hist_ref[...]
        x = x_ref[...]
        h = _rms(x, gain_ref[...]).astype(BF16)
        proj = _dot(h, win_ref[...])
        y = _mix_rows(x.reshape(bs, tb, d), proj.reshape(bs, tb, win_ref.shape[1]),
                      conv_ext_b, pool_ext_b, None, PAST_LEN, *weights)
        o_ref[...] = y.reshape(t, d)
        conv_b_ref[...] = _state_rows(conv_ext_b, CONV_PAD, tb, CONV_W - 1)
        pool_b_ref[...] = _state_rows(pool_ext_b, POOL_PAD, tb, POOL_HIST)


def _mixer(x, conv_hist, pool_hist, gain, win, convw, poolw, pscale, wout, *, group_shapes):
    (b, s, d), (nb, tb, _) = group_shapes
    in_dim = win.shape[1]
    c = in_dim // 4
    groups = len(POOL_WINDOWS)
    t = MIX_TOKENS
    bs = t // tb
    assert s % t == 0 and tb == SUBLANES and nb % bs == 0
    n_prompt_steps = b * s // t
    n_steps = n_prompt_steps + nb // bs
    assert x.shape == (n_steps * t, d)
    flat_spec = pl.BlockSpec((t, d), lambda i: (i, 0))

    def prompt_state(rows):
        return pl.BlockSpec(
            (1, rows, c), lambda i: (jnp.minimum(i, n_prompt_steps - 1) // (s // t), 0, 0))

    def sample_state(rows):
        return pl.BlockSpec((bs, rows, c), lambda i: (jnp.maximum(i - n_prompt_steps, 0), 0, 0))

    def state_shape(n, rows):
        return jax.ShapeDtypeStruct((n, rows, c), F32)

    def ext_scratch(lead, rows):
        return pltpu.VMEM(lead + (rows, c), F32)

    scratch = ([ext_scratch((), CONV_PAD + t)] + [ext_scratch((), POOL_PAD + t)] * 4
               + [ext_scratch((bs,), CONV_PAD + tb), ext_scratch((bs,), POOL_PAD + tb)])
    weight_bytes = (d * in_dim + 2 * c * d + c * c) * win.dtype.itemsize
    scratch_bytes = (5 * (POOL_PAD + t) + 2 * bs * (POOL_PAD + tb)) * c * 4
    vmem = weight_bytes + 12 * t * d * 4 + scratch_bytes
    assert vmem < V7X_VMEM_BYTES
    return pl.pallas_call(
        functools.partial(_mixer_kernel, n_prompt_steps=n_prompt_steps, steps_per_seq=s // t),
        grid=(n_steps,),
        in_specs=[flat_spec, sample_state(CONV_W - 1), sample_state(POOL_HIST),
                  _resident((1, d)), _resident((d, in_dim)), _resident((CONV_W, c)),
                  _resident((groups, c // groups, c // groups)), _resident((1, c)),
                  _resident((2 * c, d))],
        out_specs=[flat_spec, prompt_state(CONV_W - 1), prompt_state(POOL_HIST),
                   sample_state(CONV_W - 1), sample_state(POOL_HIST)],
        out_shape=[jax.ShapeDtypeStruct(x.shape, F32),
                   state_shape(b, CONV_W - 1), state_shape(b, POOL_HIST),
                   state_shape(nb, CONV_W - 1), state_shape(nb, POOL_HIST)],
        scratch_shapes=scratch,
        compiler_params=pltpu.CompilerParams(
            dimension_semantics=("arbitrary",), vmem_limit_bytes=vmem),
        name="mixer",
    )(x, conv_hist, pool_hist, gain, win, convw, poolw, pscale, wout)


def kernel(x_prompt, x_sample, state_conv, state_pool, norm_ffn1, ffn1_gate, ffn1_up, ffn1_down,
           norm_mix, w_in, conv_w, pool_w, pool_scale, w_out, norm_ffn2, ffn2_gate, ffn2_up,
           ffn2_down, norm_final):
    row = lambda v: v.reshape(1, -1)
    gain_final = row(norm_final)
    group_shapes = (x_prompt.shape, x_sample.shape)
    x = _ffn((x_prompt, x_sample), row(norm_ffn1), ffn1_gate.astype(BF16), ffn1_up.astype(BF16),
             ffn1_down.astype(BF16), gain_final,
             group_shapes=group_shapes, split_in=True, split_out=False, final_norm=False,
             name="ffn1")
    x, cp, pp, cs, ps = _mixer(x, state_conv, state_pool, row(norm_mix), w_in, conv_w, pool_w,
                               row(pool_scale), w_out, group_shapes=group_shapes)
    yp, ys = _ffn(x, row(norm_ffn2), ffn2_gate.astype(BF16), ffn2_up.astype(BF16),
                  ffn2_down.astype(BF16), gain_final,
                  group_shapes=group_shapes, split_in=False, split_out=True, final_norm=True,
                  name="ffn2")
    return (yp, ys, cp, pp, cs, ps)
```

```python
import functools

import jax
import jax.numpy as jnp
from jax.experimental import pallas as pl
from jax.experimental.pallas import tpu as pltpu

EPS = 1e-6
PAST_LEN = 16384
CONV_W = 3
POOL_WINDOWS = (2, 4, 8, 16)
POOL_HIST = max(POOL_WINDOWS) - 1
SUBLANES = 8
POOL_PAD = 24
CONV_PAD = 8
MXU_COLS = 256
V7X_VMEM_BYTES = 64 * 1024 * 1024
FFN_TOKENS = 512
FFN_CHUNK = 768
MIX_TOKENS = 512

BF16 = jnp.bfloat16
F32 = jnp.float32


def _rms(x, gain):
    inv = jax.lax.rsqrt(jnp.mean(x * x, axis=-1, keepdims=True) + EPS)
    return x * inv * gain


def _dot(a, b):
    return jax.lax.dot_general(a, b, (((a.ndim - 1,), (0,)), ((), ())), preferred_element_type=F32)


def _resident(shape):
    return pl.BlockSpec(shape, lambda *_: (0,) * len(shape), pipeline_mode=pl.Buffered(1))


def _group_specs(n_prompt_steps, steps_per_seq, seqs_per_step, t, d):
    def prompt_map(i):
        j = jnp.minimum(i, n_prompt_steps - 1)
        return (j // steps_per_seq, j % steps_per_seq, 0)

    def sample_map(i):
        return (jnp.maximum(i - n_prompt_steps, 0), 0, 0)

    return (pl.BlockSpec((1, t, d), prompt_map),
            pl.BlockSpec((seqs_per_step, SUBLANES, d), sample_map))


def _ffn_kernel(*refs, n_prompt_steps, split_in, split_out, final_norm):
    n_in = 2 if split_in else 1
    x_refs, (gain_ref, wg_ref, wu_ref, wd_ref, gain_out_ref), o_refs = (
        refs[:n_in], refs[n_in:n_in + 5], refs[n_in + 5:])
    on_prompt = pl.program_id(0) < n_prompt_steps
    if split_in:
        xa_ref, xb_ref = x_refs
        x = jnp.where(on_prompt, xa_ref[0], xb_ref[...].reshape(xa_ref.shape[1:]))
    else:
        x = x_refs[0][...]
    h = _rms(x, gain_ref[...]).astype(BF16)
    d_ff = wg_ref.shape[1]
    acc = jnp.zeros(x.shape, F32)
    for c0 in range(0, d_ff, FFN_CHUNK):
        cols = slice(c0, min(c0 + FFN_CHUNK, d_ff))
        g = _dot(h, wg_ref[:, cols])
        u = _dot(h, wu_ref[:, cols])
        a = (g * jax.nn.sigmoid(g) * u).astype(BF16)
        acc = acc + _dot(a, wd_ref[cols, :])
    y = x + 0.5 * acc
    if final_norm:
        y = _rms(y, gain_out_ref[...])
    if split_out:
        oa_ref, ob_ref = o_refs

        @pl.when(on_prompt)
        def _():
            oa_ref[0] = y

        @pl.when(jnp.logical_not(on_prompt))
        def _():
            ob_ref[...] = y.reshape(ob_ref.shape)
    else:
        o_refs[0][...] = y


def _ffn(x, gain, wg, wu, wd, gain_out, *, group_shapes, split_in, split_out, final_norm, name):
    (b, s, d), (nb, tb, _) = group_shapes
    d_ff = wg.shape[1]
    t = FFN_TOKENS
    assert s % t == 0 and tb == SUBLANES and (nb * tb) % t == 0 and d_ff % MXU_COLS == 0
    n_prompt_steps = b * s // t
    n_steps = n_prompt_steps + nb * tb // t
    pair_specs = _group_specs(n_prompt_steps, s // t, t // tb, t, d)
    flat_spec = pl.BlockSpec((t, d), lambda i: (i, 0))
    pair_shapes = [jax.ShapeDtypeStruct(shape, F32) for shape in group_shapes]
    flat_shape = jax.ShapeDtypeStruct((n_steps * t, d), F32)
    assert FFN_CHUNK % MXU_COLS == 0
    vmem = 3 * d * d_ff * wg.dtype.itemsize + 12 * t * d * 4
    assert vmem < V7X_VMEM_BYTES
    return pl.pallas_call(
        functools.partial(_ffn_kernel, n_prompt_steps=n_prompt_steps, split_in=split_in,
                          split_out=split_out, final_norm=final_norm),
        grid=(n_steps,),
        in_specs=[*(pair_specs if split_in else (flat_spec,)),
                  _resident((1, d)), _resident((d, d_ff)), _resident((d, d_ff)),
                  _resident((d_ff, d)), _resident((1, d))],
        out_specs=list(pair_specs) if split_out else flat_spec,
        out_shape=pair_shapes if split_out else flat_shape,
        compiler_params=pltpu.CompilerParams(
            dimension_semantics=("arbitrary",), vmem_limit_bytes=vmem),
        name=name,
    )(*(x if split_in else (x,)), gain, wg, wu, wd, gain_out)


def _rows(lead, start, size, lanes=slice(None)):
    return lead + (slice(start, start + size), lanes)


def _mix_rows(x, proj, conv_ext, pool_ext, pool_lvl, first_pos,
              convw_ref, poolw_ref, pscale_ref, wout_ref):
    c = conv_ext.shape[-1]
    t = x.shape[-2]
    lead = (slice(None),) * (x.ndim - 2)
    gb = proj[..., 0:c]
    gc = proj[..., c:2 * c]
    u = proj[..., 2 * c:3 * c]
    p = proj[..., 3 * c:4 * c]

    conv_ext[_rows(lead, CONV_PAD, t)] = gc * u
    conv = None
    for k in range(CONV_W):
        term = conv_ext[_rows(lead, CONV_PAD - (CONV_W - 1) + k, t)] * convw_ref[k:k + 1, :]
        conv = term if conv is None else conv + term
    y_conv = gb * conv

    pool_ext[_rows(lead, POOL_PAD, t)] = p
    gdim = c // len(POOL_WINDOWS)
    wsums = []
    if pool_lvl is None:
        for g, w in enumerate(POOL_WINDOWS):
            lanes = slice(g * gdim, (g + 1) * gdim)
            wsum = pool_ext[_rows(lead, POOL_PAD, t, lanes)]
            for j in range(1, w):
                wsum = wsum + pool_ext[_rows(lead, POOL_PAD - j, t, lanes)]
            wsums.append(wsum)
    else:
        span = POOL_PAD - SUBLANES + t
        src = pool_ext
        for g, w in enumerate(POOL_WINDOWS):
            lanes = slice(g * gdim, c)
            shift = w // 2
            if g < len(pool_lvl):
                dst = pool_lvl[g]
                dst[_rows(lead, SUBLANES, span, lanes)] = (
                    src[_rows(lead, SUBLANES, span, lanes)]
                    + src[_rows(lead, SUBLANES - shift, span, lanes)])
                wsums.append(dst[_rows(lead, POOL_PAD, t, slice(g * gdim, (g + 1) * gdim))])
                src = dst
            else:
                wsums.append(src[_rows(lead, POOL_PAD, t, lanes)]
                             + src[_rows(lead, POOL_PAD - shift, t, lanes)])

    pos = first_pos + jax.lax.broadcasted_iota(jnp.int32, (t, 1), 0)
    y_pool = []
    for g, w in enumerate(POOL_WINDOWS):
        lanes = slice(g * gdim, (g + 1) * gdim)
        cnt = jnp.minimum(pos + 1, w).astype(F32)
        dlt = (wsums[g] / cnt - p[..., lanes]).astype(BF16)
        y = _dot(dlt.reshape(-1, gdim), poolw_ref[g]).reshape(dlt.shape)
        y_pool.append(y * pscale_ref[:, lanes])
    mix = jnp.concatenate([y_conv] + y_pool, axis=-1).astype(BF16)
    out = _dot(mix.reshape(-1, 2 * c), wout_ref[...])
    return x + out.reshape(x.shape)


def _state_rows(ext, pad, t, n):
    lead = (slice(None),) * (len(ext.shape) - 2)
    return ext[_rows(lead, pad + t - n, n)]


def _mixer_kernel(x_ref, conv_hist_ref, pool_hist_ref, gain_ref, win_ref, convw_ref,
                  poolw_ref, pscale_ref, wout_ref,
                  o_ref, conv_a_ref, pool_a_ref, conv_b_ref, pool_b_ref,
                  conv_ext_a, pool_ext_a, lvl1_a, lvl2_a, lvl3_a, conv_ext_b, pool_ext_b,
                  *, n_prompt_steps, steps_per_seq):
    i = pl.program_id(0)
    t, d = x_ref.shape
    weights = (convw_ref, poolw_ref, pscale_ref, wout_ref)

    @pl.when(i == 0)
    def _():
        for ref in (pool_ext_a, lvl1_a, lvl2_a):
            ref[0:POOL_PAD - POOL_HIST, :] = jnp.zeros((POOL_PAD - POOL_HIST, ref.shape[-1]), F32)

    @pl.when(i < n_prompt_steps)
    def _():
        ti = i % steps_per_seq

        @pl.when(ti == 0)
        def _():
            conv_ext_a[0:CONV_PAD, :] = jnp.zeros((CONV_PAD, conv_ext_a.shape[1]), F32)
            pool_ext_a[SUBLANES:POOL_PAD, :] = jnp.zeros(
                (POOL_PAD - SUBLANES, pool_ext_a.shape[1]), F32)

        @pl.when(ti > 0)
        def _():
            conv_ext_a[0:CONV_PAD, :] = conv_ext_a[t:t + CONV_PAD, :]
            pool_ext_a[SUBLANES:POOL_PAD, :] = pool_ext_a[t + SUBLANES:t + POOL_PAD, :]

        x = x_ref[...]
        h = _rms(x, gain_ref[...]).astype(BF16)
        proj = _dot(h, win_ref[...])
        o_ref[...] = _mix_rows(x, proj, conv_ext_a, pool_ext_a, (lvl1_a, lvl2_a, lvl3_a),
                               ti * t, *weights)
        conv_a_ref[0] = _state_rows(conv_ext_a, CONV_PAD, t, CONV_W - 1)
        pool_a_ref[0] = _state_rows(pool_ext_a, POOL_PAD, t, POOL_HIST)

    @pl.when(i >= n_prompt_steps)
    def _():
        bs, tb = conv_ext_b.shape[0], conv_ext_b.shape[1] - CONV_PAD
        conv_ext_b[:, CONV_PAD - (CONV_W - 1):CONV_PAD, :] = conv_hist_ref[...]
        pool_ext_b[:, POOL_PAD - POOL_HIST:POOL_PAD, :] = pool_hist_ref[...]
        x = x_ref[...]
        h = _rms(x, gain_ref[...]).astype(BF16)
        proj = _dot(h, win_ref[...])
        y = _mix_rows(x.reshape(bs, tb, d), proj.reshape(bs, tb, win_ref.shape[1]),
                      conv_ext_b, pool_ext_b, None, PAST_LEN, *weights)
        o_ref[...] = y.reshape(t, d)
        conv_b_ref[...] = _state_rows(conv_ext_b, CONV_PAD, tb, CONV_W - 1)
        pool_b_ref[...] = _state_rows(pool_ext_b, POOL_PAD, tb, POOL_HIST)


def _mixer(x, conv_hist, pool_hist, gain, win, convw, poolw, pscale, wout, *, group_shapes):
    (b, s, d), (nb, tb, _) = group_shapes
    in_dim = win.shape[1]
    c = in_dim // 4
    groups = len(POOL_WINDOWS)
    t = MIX_TOKENS
    bs = t // tb
    assert s % t == 0 and tb == SUBLANES and nb % bs == 0
    n_prompt_steps = b * s // t
    n_steps = n_prompt_steps + nb // bs
    assert x.shape == (n_steps * t, d)
    flat_spec = pl.BlockSpec((t, d), lambda i: (i, 0))

    def prompt_state(rows):
        return pl.BlockSpec(
            (1, rows, c), lambda i: (jnp.minimum(i, n_prompt_steps - 1) // (s // t), 0, 0))

    def sample_state(rows):
        return pl.BlockSpec((bs, rows, c), lambda i: (jnp.maximum(i - n_prompt_steps, 0), 0, 0))

    def state_shape(n, rows):
        return jax.ShapeDtypeStruct((n, rows, c), F32)

    def ext_scratch(lead, rows):
        return pltpu.VMEM(lead + (rows, c), F32)

    scratch = ([ext_scratch((), CONV_PAD + t)] + [ext_scratch((), POOL_PAD + t)] * 4
               + [ext_scratch((bs,), CONV_PAD + tb), ext_scratch((bs,), POOL_PAD + tb)])
    weight_bytes = (d * in_dim + 2 * c * d + c * c) * win.dtype.itemsize
    scratch_bytes = (5 * (POOL_PAD + t) + 2 * bs * (POOL_PAD + tb)) * c * 4
    vmem = weight_bytes + 12 * t * d * 4 + scratch_bytes
    assert vmem < V7X_VMEM_BYTES
    return pl.pallas_call(
        functools.partial(_mixer_kernel, n_prompt_steps=n_prompt_steps, steps_per_seq=s // t),
        grid=(n_steps,),
        in_specs=[flat_spec, sample_state(CONV_W - 1), sample_state(POOL_HIST),
                  _resident((1, d)), _resident((d, in_dim)), _resident((CONV_W, c)),
                  _resident((groups, c // groups, c // groups)), _resident((1, c)),
                  _resident((2 * c, d))],
        out_specs=[flat_spec, prompt_state(CONV_W - 1), prompt_state(POOL_HIST),
                   sample_state(CONV_W - 1), sample_state(POOL_HIST)],
        out_shape=[jax.ShapeDtypeStruct(x.shape, F32),
                   state_shape(b, CONV_W - 1), state_shape(b, POOL_HIST),
                   state_shape(nb, CONV_W - 1), state_shape(nb, POOL_HIST)],
        scratch_shapes=scratch,
        compiler_params=pltpu.CompilerParams(
            dimension_semantics=("arbitrary",), vmem_limit_bytes=vmem),
        name="mixer",
    )(x, conv_hist, pool_hist, gain, win, convw, poolw, pscale, wout)


def kernel(x_prompt, x_sample, state_conv, state_pool, norm_ffn1, ffn1_gate, ffn1_up, ffn1_down,
           norm_mix, w_in, conv_w, pool_w, pool_scale, w_out, norm_ffn2, ffn2_gate, ffn2_up,
           ffn2_down, norm_final):
    row = lambda v: v.reshape(1, -1)
    gain_final = row(norm_final)
    group_shapes = (x_prompt.shape, x_sample.shape)
    x = _ffn((x_prompt, x_sample), row(norm_ffn1), ffn1_gate, ffn1_up, ffn1_down, gain_final,
             group_shapes=group_shapes, split_in=True, split_out=False, final_norm=False,
             name="ffn1")
    x, cp, pp, cs, ps = _mixer(x, state_conv, state_pool, row(norm_mix), w_in, conv_w, pool_w,
                               row(pool_scale), w_out, group_shapes=group_shapes)
    yp, ys = _ffn(x, row(norm_ffn2), ffn2_gate, ffn2_up, ffn2_down, gain_final,
                  group_shapes=group_shapes, split_in=False, split_out=True, final_norm=True,
                  name="ffn2")
    return (yp, ys, cp, pp, cs, ps)
```

```python
import functools

import jax
import jax.numpy as jnp
from jax.experimental import pallas as pl
from jax.experimental.pallas import tpu as pltpu

EPS = 1e-6
PAST_LEN = 16384
CONV_W = 3
POOL_WINDOWS = (2, 4, 8, 16)
POOL_HIST = max(POOL_WINDOWS) - 1
SUBLANES = 8
POOL_PAD = 24
CONV_PAD = 8
MXU_COLS = 256
V7X_VMEM_BYTES = 64 * 1024 * 1024
FFN_TOKENS = 512
FFN_CHUNK = 256
MIX_TOKENS = 512

BF16 = jnp.bfloat16
F32 = jnp.float32


def _rms(x, gain):
    inv = jax.lax.rsqrt(jnp.mean(x * x, axis=-1, keepdims=True) + EPS)
    return x * inv * gain


def _dot(a, b):
    return jax.lax.dot_general(a, b, (((a.ndim - 1,), (0,)), ((), ())), preferred_element_type=F32)


def _resident(shape):
    return pl.BlockSpec(shape, lambda *_: (0,) * len(shape), pipeline_mode=pl.Buffered(1))


def _group_specs(n_prompt_steps, steps_per_seq, seqs_per_step, t, d):
    def prompt_map(i):
        j = jnp.minimum(i, n_prompt_steps - 1)
        return (j // steps_per_seq, j % steps_per_seq, 0)

    def sample_map(i):
        return (jnp.maximum(i - n_prompt_steps, 0), 0, 0)

    return (pl.BlockSpec((1, t, d), prompt_map),
            pl.BlockSpec((seqs_per_step, SUBLANES, d), sample_map))


def _ffn_kernel(*refs, n_prompt_steps, split_in, split_out, final_norm):
    n_in = 2 if split_in else 1
    x_refs, (gain_ref, wg_ref, wu_ref, wd_ref, gain_out_ref), o_refs = (
        refs[:n_in], refs[n_in:n_in + 5], refs[n_in + 5:])
    on_prompt = pl.program_id(0) < n_prompt_steps
    if split_in:
        xa_ref, xb_ref = x_refs
        x = jnp.where(on_prompt, xa_ref[0], xb_ref[...].reshape(xa_ref.shape[1:]))
    else:
        x = x_refs[0][...]
    h = _rms(x, gain_ref[...]).astype(BF16)
    d_ff = wg_ref.shape[1]
    act = []
    for c0 in range(0, d_ff, FFN_CHUNK):
        cols = slice(c0, min(c0 + FFN_CHUNK, d_ff))
        g = _dot(h, wg_ref[:, cols])
        u = _dot(h, wu_ref[:, cols])
        act.append((g * jax.nn.sigmoid(g) * u).astype(BF16))
    y = x + 0.5 * _dot(jnp.concatenate(act, axis=-1), wd_ref[...])
    if final_norm:
        y = _rms(y, gain_out_ref[...])
    if split_out:
        oa_ref, ob_ref = o_refs

        @pl.when(on_prompt)
        def _():
            oa_ref[0] = y

        @pl.when(jnp.logical_not(on_prompt))
        def _():
            ob_ref[...] = y.reshape(ob_ref.shape)
    else:
        o_refs[0][...] = y


def _ffn(x, gain, wg, wu, wd, gain_out, *, group_shapes, split_in, split_out, final_norm, name):
    (b, s, d), (nb, tb, _) = group_shapes
    d_ff = wg.shape[1]
    t = FFN_TOKENS
    assert s % t == 0 and tb == SUBLANES and (nb * tb) % t == 0 and d_ff % MXU_COLS == 0
    n_prompt_steps = b * s // t
    n_steps = n_prompt_steps + nb * tb // t
    pair_specs = _group_specs(n_prompt_steps, s // t, t // tb, t, d)
    flat_spec = pl.BlockSpec((t, d), lambda i: (i, 0))
    pair_shapes = [jax.ShapeDtypeStruct(shape, F32) for shape in group_shapes]
    flat_shape = jax.ShapeDtypeStruct((n_steps * t, d), F32)
    assert FFN_CHUNK % MXU_COLS == 0
    vmem = 3 * d * d_ff * wg.dtype.itemsize + 12 * t * d * 4
    assert vmem < V7X_VMEM_BYTES
    return pl.pallas_call(
        functools.partial(_ffn_kernel, n_prompt_steps=n_prompt_steps, split_in=split_in,
                          split_out=split_out, final_norm=final_norm),
        grid=(n_steps,),
        in_specs=[*(pair_specs if split_in else (flat_spec,)),
                  _resident((1, d)), _resident((d, d_ff)), _resident((d, d_ff)),
                  _resident((d_ff, d)), _resident((1, d))],
        out_specs=list(pair_specs) if split_out else flat_spec,
        out_shape=pair_shapes if split_out else flat_shape,
        compiler_params=pltpu.CompilerParams(
            dimension_semantics=("arbitrary",), vmem_limit_bytes=vmem),
        name=name,
    )(*(x if split_in else (x,)), gain, wg, wu, wd, gain_out)


def _rows(lead, start, size, lanes=slice(None)):
    return lead + (slice(start, start + size), lanes)


def _mix_rows(x, proj, conv_ext, pool_ext, pool_lvl, first_pos,
              convw_ref, poolw_ref, pscale_ref, wout_ref):
    c = conv_ext.shape[-1]
    t = x.shape[-2]
    lead = (slice(None),) * (x.ndim - 2)
    gb = proj[..., 0:c]
    gc = proj[..., c:2 * c]
    u = proj[..., 2 * c:3 * c]
    p = proj[..., 3 * c:4 * c]

    conv_ext[_rows(lead, CONV_PAD, t)] = gc * u
    conv = None
    for k in range(CONV_W):
        term = conv_ext[_rows(lead, CONV_PAD - (CONV_W - 1) + k, t)] * convw_ref[k:k + 1, :]
        conv = term if conv is None else conv + term
    y_conv = gb * conv

    pool_ext[_rows(lead, POOL_PAD, t)] = p
    gdim = c // len(POOL_WINDOWS)
    wsums = []
    if pool_lvl is None:
        for g, w in enumerate(POOL_WINDOWS):
            lanes = slice(g * gdim, (g + 1) * gdim)
            wsum = pool_ext[_rows(lead, POOL_PAD, t, lanes)]
            for j in range(1, w):
                wsum = wsum + pool_ext[_rows(lead, POOL_PAD - j, t, lanes)]
            wsums.append(wsum)
    else:
        span = POOL_PAD - SUBLANES + t
        src = pool_ext
        for g, w in enumerate(POOL_WINDOWS):
            lanes = slice(g * gdim, c)
            shift = w // 2
            if g < len(pool_lvl):
                dst = pool_lvl[g]
                dst[_rows(lead, SUBLANES, span, lanes)] = (
                    src[_rows(lead, SUBLANES, span, lanes)]
                    + src[_rows(lead, SUBLANES - shift, span, lanes)])
                wsums.append(dst[_rows(lead, POOL_PAD, t, slice(g * gdim, (g + 1) * gdim))])
                src = dst
            else:
                wsums.append(src[_rows(lead, POOL_PAD, t, lanes)]
                             + src[_rows(lead, POOL_PAD - shift, t, lanes)])

    pos = first_pos + jax.lax.broadcasted_iota(jnp.int32, (t, 1), 0)
    y_pool = []
    for g, w in enumerate(POOL_WINDOWS):
        lanes = slice(g * gdim, (g + 1) * gdim)
        cnt = jnp.minimum(pos + 1, w).astype(F32)
        dlt = (wsums[g] / cnt - p[..., lanes]).astype(BF16)
        y = _dot(dlt.reshape(-1, gdim), poolw_ref[g]).reshape(dlt.shape)
        y_pool.append(y * pscale_ref[:, lanes])
    mix = jnp.concatenate([y_conv] + y_pool, axis=-1).astype(BF16)
    out = _dot(mix.reshape(-1, 2 * c), wout_ref[...])
    return x + out.reshape(x.shape)


def _state_rows(ext, pad, t, n):
    lead = (slice(None),) * (len(ext.shape) - 2)
    return ext[_rows(lead, pad + t - n, n)]


def _mixer_kernel(x_ref, conv_hist_ref, pool_hist_ref, gain_ref, win_ref, convw_ref,
                  poolw_ref, pscale_ref, wout_ref,
                  o_ref, conv_a_ref, pool_a_ref, conv_b_ref, pool_b_ref,
                  conv_ext_a, pool_ext_a, lvl1_a, lvl2_a, lvl3_a, conv_ext_b, pool_ext_b,
                  *, n_prompt_steps, steps_per_seq):
    i = pl.program_id(0)
    t, d = x_ref.shape
    weights = (convw_ref, poolw_ref, pscale_ref, wout_ref)

    @pl.when(i == 0)
    def _():
        for ref in (pool_ext_a, lvl1_a, lvl2_a):
            ref[0:POOL_PAD - POOL_HIST, :] = jnp.zeros((POOL_PAD - POOL_HIST, ref.shape[-1]), F32)

    @pl.when(i < n_prompt_steps)
    def _():
        ti = i % steps_per_seq

        @pl.when(ti == 0)
        def _():
            conv_ext_a[0:CONV_PAD, :] = jnp.zeros((CONV_PAD, conv_ext_a.shape[1]), F32)
            pool_ext_a[SUBLANES:POOL_PAD, :] = jnp.zeros(
                (POOL_PAD - SUBLANES, pool_ext_a.shape[1]), F32)

        @pl.when(ti > 0)
        def _():
            conv_ext_a[0:CONV_PAD, :] = conv_ext_a[t:t + CONV_PAD, :]
            pool_ext_a[SUBLANES:POOL_PAD, :] = pool_ext_a[t + SUBLANES:t + POOL_PAD, :]

        x = x_ref[...]
        h = _rms(x, gain_ref[...]).astype(BF16)
        proj = _dot(h, win_ref[...])
        o_ref[...] = _mix_rows(x, proj, conv_ext_a, pool_ext_a, (lvl1_a, lvl2_a, lvl3_a),
                               ti * t, *weights)
        conv_a_ref[0] = _state_rows(conv_ext_a, CONV_PAD, t, CONV_W - 1)
        pool_a_ref[0] = _state_rows(pool_ext_a, POOL_PAD, t, POOL_HIST)

    @pl.when(i >= n_prompt_steps)
    def _():
        bs, tb = conv_ext_b.shape[0], conv_ext_b.shape[1] - CONV_PAD
        conv_ext_b[:, CONV_PAD - (CONV_W - 1):CONV_PAD, :] = conv_hist_ref[...]
        pool_ext_b[:, POOL_PAD - POOL_HIST:POOL_PAD, :] = pool_hist_ref[...]
        x = x_ref[...]
        h = _rms(x, gain_ref[...]).astype(BF16)
        proj = _dot(h, win_ref[...])
        y = _mix_rows(x.reshape(bs, tb, d), proj.reshape(bs, tb, win_ref.shape[1]),
                      conv_ext_b, pool_ext_b, None, PAST_LEN, *weights)
        o_ref[...] = y.reshape(t, d)
        conv_b_ref[...] = _state_rows(conv_ext_b, CONV_PAD, tb, CONV_W - 1)
        pool_b_ref[...] = _state_rows(pool_ext_b, POOL_PAD, tb, POOL_HIST)


def _mixer(x, conv_hist, pool_hist, gain, win, convw, poolw, pscale, wout, *, group_shapes):
    (b, s, d), (nb, tb, _) = group_shapes
    in_dim = win.shape[1]
    c = in_dim // 4
    groups = len(POOL_WINDOWS)
    t = MIX_TOKENS
    bs = t // tb
    assert s % t == 0 and tb == SUBLANES and nb % bs == 0
    n_prompt_steps = b * s // t
    n_steps = n_prompt_steps + nb // bs
    assert x.shape == (n_steps * t, d)
    flat_spec = pl.BlockSpec((t, d), lambda i: (i, 0))

    def prompt_state(rows):
        return pl.BlockSpec(
            (1, rows, c), lambda i: (jnp.minimum(i, n_prompt_steps - 1) // (s // t), 0, 0))

    def sample_state(rows):
        return pl.BlockSpec((bs, rows, c), lambda i: (jnp.maximum(i - n_prompt_steps, 0), 0, 0))

    def state_shape(n, rows):
        return jax.ShapeDtypeStruct((n, rows, c), F32)

    def ext_scratch(lead, rows):
        return pltpu.VMEM(lead + (rows, c), F32)

    scratch = ([ext_scratch((), CONV_PAD + t)] + [ext_scratch((), POOL_PAD + t)] * 4
               + [ext_scratch((bs,), CONV_PAD + tb), ext_scratch((bs,), POOL_PAD + tb)])
    weight_bytes = (d * in_dim + 2 * c * d + c * c) * win.dtype.itemsize
    scratch_bytes = (5 * (POOL_PAD + t) + 2 * bs * (POOL_PAD + tb)) * c * 4
    vmem = weight_bytes + 12 * t * d * 4 + scratch_bytes
    assert vmem < V7X_VMEM_BYTES
    return pl.pallas_call(
        functools.partial(_mixer_kernel, n_prompt_steps=n_prompt_steps, steps_per_seq=s // t),
        grid=(n_steps,),
        in_specs=[flat_spec, sample_state(CONV_W - 1), sample_state(POOL_HIST),
                  _resident((1, d)), _resident((d, in_dim)), _resident((CONV_W, c)),
                  _resident((groups, c // groups, c // groups)), _resident((1, c)),
                  _resident((2 * c, d))],
        out_specs=[flat_spec, prompt_state(CONV_W - 1), prompt_state(POOL_HIST),
                   sample_state(CONV_W - 1), sample_state(POOL_HIST)],
        out_shape=[jax.ShapeDtypeStruct(x.shape, F32),
                   state_shape(b, CONV_W - 1), state_shape(b, POOL_HIST),
                   state_shape(nb, CONV_W - 1), state_shape(nb, POOL_HIST)],
        scratch_shapes=scratch,
        compiler_params=pltpu.CompilerParams(
            dimension_semantics=("arbitrary",), vmem_limit_bytes=vmem),
        name="mixer",
    )(x, conv_hist, pool_hist, gain, win, convw, poolw, pscale, wout)


def kernel(x_prompt, x_sample, state_conv, state_pool, norm_ffn1, ffn1_gate, ffn1_up, ffn1_down,
           norm_mix, w_in, conv_w, pool_w, pool_scale, w_out, norm_ffn2, ffn2_gate, ffn2_up,
           ffn2_down, norm_final):
    row = lambda v: v.reshape(1, -1)
    gain_final = row(norm_final)
    group_shapes = (x_prompt.shape, x_sample.shape)
    x = _ffn((x_prompt, x_sample), row(norm_ffn1), ffn1_gate, ffn1_up, ffn1_down, gain_final,
             group_shapes=group_shapes, split_in=True, split_out=False, final_norm=False,
             name="ffn1")
    x, cp, pp, cs, ps = _mixer(x, state_conv, state_pool, row(norm_mix), w_in, conv_w, pool_w,
                               row(pool_scale), w_out, group_shapes=group_shapes)
    yp, ys = _ffn(x, row(norm_ffn2), ffn2_gate, ffn2_up, ffn2_down, gain_final,
                  group_shapes=group_shapes, split_in=False, split_out=True, final_norm=True,
                  name="ffn2")
    return (yp, ys, cp, pp, cs, ps)
```

```python
import functools

import jax
import jax.numpy as jnp
from jax.experimental import pallas as pl
from jax.experimental.pallas import tpu as pltpu

EPS = 1e-6
PAST_LEN = 16384
CONV_W = 3
POOL_WINDOWS = (2, 4, 8, 16)
POOL_HIST = max(POOL_WINDOWS) - 1
SUBLANES = 8
POOL_PAD = 24
CONV_PAD = 8
MXU_COLS = 256
V7X_VMEM_BYTES = 64 * 1024 * 1024
FFN_TOKENS = 512
FFN_CHUNK = 256
MIX_TOKENS = 512

BF16 = jnp.bfloat16
F32 = jnp.float32


def _rms(x, gain):
    inv = jax.lax.rsqrt(jnp.mean(x * x, axis=-1, keepdims=True) + EPS)
    return x * inv * gain


def _dot(a, b):
    return jax.lax.dot_general(a, b, (((a.ndim - 1,), (0,)), ((), ())), preferred_element_type=F32)


def _resident(shape):
    return pl.BlockSpec(shape, lambda *_: (0,) * len(shape), pipeline_mode=pl.Buffered(1))


def _group_specs(n_prompt_steps, steps_per_seq, seqs_per_step, t, d):
    def prompt_map(i):
        j = jnp.minimum(i, n_prompt_steps - 1)
        return (j // steps_per_seq, j % steps_per_seq, 0)

    def sample_map(i):
        return (jnp.maximum(i - n_prompt_steps, 0), 0, 0)

    return (pl.BlockSpec((1, t, d), prompt_map),
            pl.BlockSpec((seqs_per_step, SUBLANES, d), sample_map))


def _ffn_kernel(*refs, n_prompt_steps, split_in, split_out, final_norm):
    n_in = 2 if split_in else 1
    n_out = 2 if split_out else 1
    x_refs = refs[:n_in]
    gain_ref, wg_hbm, wu_hbm, wd_hbm, gain_out_ref = refs[n_in:n_in + 5]
    o_refs = refs[n_in + 5:n_in + 5 + n_out]
    wg_ref, wu_ref, wd_ref, sems = refs[n_in + 5 + n_out:]
    step = pl.program_id(0)
    on_prompt = step < n_prompt_steps
    d_ff = wg_ref.shape[1]
    chunks = [slice(c0, c0 + FFN_CHUNK) for c0 in range(0, d_ff, FFN_CHUNK)]

    def weight_copies(c):
        cols = chunks[c]
        return (pltpu.make_async_copy(wg_hbm.at[:, cols], wg_ref.at[:, cols], sems.at[0, c]),
                pltpu.make_async_copy(wu_hbm.at[:, cols], wu_ref.at[:, cols], sems.at[1, c]),
                pltpu.make_async_copy(wd_hbm.at[cols, :], wd_ref.at[cols, :], sems.at[2, c]))

    def tile(first_step):
        if split_in:
            xa_ref, xb_ref = x_refs
            x = jnp.where(on_prompt, xa_ref[0], xb_ref[...].reshape(xa_ref.shape[1:]))
        else:
            x = x_refs[0][...]
        h = _rms(x, gain_ref[...]).astype(BF16)
        act = []
        for c, cols in enumerate(chunks):
            if first_step:
                gate_copy, up_copy, _ = weight_copies(c)
                gate_copy.wait()
                up_copy.wait()
            g = _dot(h, wg_ref[:, cols])
            u = _dot(h, wu_ref[:, cols])
            act.append((g * jax.nn.sigmoid(g) * u).astype(BF16))
        if first_step:
            for c in range(len(chunks)):
                weight_copies(c)[2].wait()
        y = x + 0.5 * _dot(jnp.concatenate(act, axis=-1), wd_ref[...])
        if final_norm:
            y = _rms(y, gain_out_ref[...])
        if split_out:
            oa_ref, ob_ref = o_refs

            @pl.when(on_prompt)
            def _():
                oa_ref[0] = y

            @pl.when(jnp.logical_not(on_prompt))
            def _():
                ob_ref[...] = y.reshape(ob_ref.shape)
        else:
            o_refs[0][...] = y

    @pl.when(step == 0)
    def _():
        for c in range(len(chunks)):
            gate_copy, up_copy, _ = weight_copies(c)
            gate_copy.start()
            up_copy.start()
        for c in range(len(chunks)):
            weight_copies(c)[2].start()
        tile(first_step=True)

    @pl.when(step > 0)
    def _():
        tile(first_step=False)


def _ffn(x, gain, wg, wu, wd, gain_out, *, group_shapes, split_in, split_out, final_norm, name):
    (b, s, d), (nb, tb, _) = group_shapes
    d_ff = wg.shape[1]
    t = FFN_TOKENS
    assert s % t == 0 and tb == SUBLANES and (nb * tb) % t == 0 and d_ff % FFN_CHUNK == 0
    assert FFN_CHUNK % MXU_COLS == 0
    n_prompt_steps = b * s // t
    n_steps = n_prompt_steps + nb * tb // t
    pair_specs = _group_specs(n_prompt_steps, s // t, t // tb, t, d)
    flat_spec = pl.BlockSpec((t, d), lambda i: (i, 0))
    pair_shapes = [jax.ShapeDtypeStruct(shape, F32) for shape in group_shapes]
    flat_shape = jax.ShapeDtypeStruct((n_steps * t, d), F32)
    in_hbm = pl.BlockSpec(memory_space=pl.ANY)
    vmem = 3 * d * d_ff * wg.dtype.itemsize + 12 * t * d * 4
    assert vmem < V7X_VMEM_BYTES
    return pl.pallas_call(
        functools.partial(_ffn_kernel, n_prompt_steps=n_prompt_steps, split_in=split_in,
                          split_out=split_out, final_norm=final_norm),
        grid=(n_steps,),
        in_specs=[*(pair_specs if split_in else (flat_spec,)),
                  _resident((1, d)), in_hbm, in_hbm, in_hbm, _resident((1, d))],
        out_specs=list(pair_specs) if split_out else flat_spec,
        out_shape=pair_shapes if split_out else flat_shape,
        scratch_shapes=[pltpu.VMEM((d, d_ff), wg.dtype), pltpu.VMEM((d, d_ff), wu.dtype),
                        pltpu.VMEM((d_ff, d), wd.dtype),
                        pltpu.SemaphoreType.DMA((3, d_ff // FFN_CHUNK))],
        compiler_params=pltpu.CompilerParams(
            dimension_semantics=("arbitrary",), vmem_limit_bytes=vmem),
        name=name,
    )(*(x if split_in else (x,)), gain, wg, wu, wd, gain_out)


def _rows(lead, start, size, lanes=slice(None)):
    return lead + (slice(start, start + size), lanes)


def _mix_rows(x, proj, conv_ext, pool_ext, pool_lvl, first_pos,
              convw_ref, poolw_ref, pscale_ref, wout_ref):
    c = conv_ext.shape[-1]
    t = x.shape[-2]
    lead = (slice(None),) * (x.ndim - 2)
    gb = proj[..., 0:c]
    gc = proj[..., c:2 * c]
    u = proj[..., 2 * c:3 * c]
    p = proj[..., 3 * c:4 * c]

    conv_ext[_rows(lead, CONV_PAD, t)] = gc * u
    conv = None
    for k in range(CONV_W):
        term = conv_ext[_rows(lead, CONV_PAD - (CONV_W - 1) + k, t)] * convw_ref[k:k + 1, :]
        conv = term if conv is None else conv + term
    y_conv = gb * conv

    pool_ext[_rows(lead, POOL_PAD, t)] = p
    gdim = c // len(POOL_WINDOWS)
    wsums = []
    if pool_lvl is None:
        for g, w in enumerate(POOL_WINDOWS):
            lanes = slice(g * gdim, (g + 1) * gdim)
            wsum = pool_ext[_rows(lead, POOL_PAD, t, lanes)]
            for j in range(1, w):
                wsum = wsum + pool_ext[_rows(lead, POOL_PAD - j, t, lanes)]
            wsums.append(wsum)
    else:
        span = POOL_PAD - SUBLANES + t
        src = pool_ext
        for g, w in enumerate(POOL_WINDOWS):
            lanes = slice(g * gdim, c)
            shift = w // 2
            if g < len(pool_lvl):
                dst = pool_lvl[g]
                dst[_rows(lead, SUBLANES, span, lanes)] = (
                    src[_rows(lead, SUBLANES, span, lanes)]
                    + src[_rows(lead, SUBLANES - shift, span, lanes)])
                wsums.append(dst[_rows(lead, POOL_PAD, t, slice(g * gdim, (g + 1) * gdim))])
                src = dst
            else:
                wsums.append(src[_rows(lead, POOL_PAD, t, lanes)]
                             + src[_rows(lead, POOL_PAD - shift, t, lanes)])

    pos = first_pos + jax.lax.broadcasted_iota(jnp.int32, (t, 1), 0)
    y_pool = []
    for g, w in enumerate(POOL_WINDOWS):
        lanes = slice(g * gdim, (g + 1) * gdim)
        cnt = jnp.minimum(pos + 1, w).astype(F32)
        dlt = (wsums[g] / cnt - p[..., lanes]).astype(BF16)
        y = _dot(dlt.reshape(-1, gdim), poolw_ref[g]).reshape(dlt.shape)
        y_pool.append(y * pscale_ref[:, lanes])
    mix = jnp.concatenate([y_conv] + y_pool, axis=-1).astype(BF16)
    out = _dot(mix.reshape(-1, 2 * c), wout_ref[...])
    return x + out.reshape(x.shape)


def _state_rows(ext, pad, t, n):
    lead = (slice(None),) * (len(ext.shape) - 2)
    return ext[_rows(lead, pad + t - n, n)]


def _mixer_kernel(x_ref, conv_hist_ref, pool_hist_ref, gain_ref, win_ref, convw_ref,
                  poolw_ref, pscale_ref, wout_ref,
                  o_ref, conv_a_ref, pool_a_ref, conv_b_ref, pool_b_ref,
                  conv_ext_a, pool_ext_a, lvl1_a, lvl2_a, lvl3_a, conv_ext_b, pool_ext_b,
                  *, n_prompt_steps, steps_per_seq):
    i = pl.program_id(0)
    t, d = x_ref.shape
    weights = (convw_ref, poolw_ref, pscale_ref, wout_ref)

    @pl.when(i == 0)
    def _():
        for ref in (pool_ext_a, lvl1_a, lvl2_a):
            ref[0:POOL_PAD - POOL_HIST, :] = jnp.zeros((POOL_PAD - POOL_HIST, ref.shape[-1]), F32)

    @pl.when(i < n_prompt_steps)
    def _():
        ti = i % steps_per_seq

        @pl.when(ti == 0)
        def _():
            conv_ext_a[0:CONV_PAD, :] = jnp.zeros((CONV_PAD, conv_ext_a.shape[1]), F32)
            pool_ext_a[SUBLANES:POOL_PAD, :] = jnp.zeros(
                (POOL_PAD - SUBLANES, pool_ext_a.shape[1]), F32)

        @pl.when(ti > 0)
        def _():
            conv_ext_a[0:CONV_PAD, :] = conv_ext_a[t:t + CONV_PAD, :]
            pool_ext_a[SUBLANES:POOL_PAD, :] = pool_ext_a[t + SUBLANES:t + POOL_PAD, :]

        x = x_ref[...]
        h = _rms(x, gain_ref[...]).astype(BF16)
        proj = _dot(h, win_ref[...])
        o_ref[...] = _mix_rows(x, proj, conv_ext_a, pool_ext_a, (lvl1_a, lvl2_a, lvl3_a),
                               ti * t, *weights)
        conv_a_ref[0] = _state_rows(conv_ext_a, CONV_PAD, t, CONV_W - 1)
        pool_a_ref[0] = _state_rows(pool_ext_a, POOL_PAD, t, POOL_HIST)

    @pl.when(i >= n_prompt_steps)
    def _():
        bs, tb = conv_ext_b.shape[0], conv_ext_b.shape[1] - CONV_PAD
        conv_ext_b[:, CONV_PAD - (CONV_W - 1):CONV_PAD, :] = conv_hist_ref[...]
        pool_ext_b[:, POOL_PAD - POOL_HIST:POOL_PAD, :] = pool_hist_ref[...]
        x = x_ref[...]
        h = _rms(x, gain_ref[...]).astype(BF16)
        proj = _dot(h, win_ref[...])
        y = _mix_rows(x.reshape(bs, tb, d), proj.reshape(bs, tb, win_ref.shape[1]),
                      conv_ext_b, pool_ext_b, None, PAST_LEN, *weights)
        o_ref[...] = y.reshape(t, d)
        conv_b_ref[...] = _state_rows(conv_ext_b, CONV_PAD, tb, CONV_W - 1)
        pool_b_ref[...] = _state_rows(pool_ext_b, POOL_PAD, tb, POOL_HIST)


def _mixer(x, conv_hist, pool_hist, gain, win, convw, poolw, pscale, wout, *, group_shapes):
    (b, s, d), (nb, tb, _) = group_shapes
    in_dim = win.shape[1]
    c = in_dim // 4
    groups = len(POOL_WINDOWS)
    t = MIX_TOKENS
    bs = t // tb
    assert s % t == 0 and tb == SUBLANES and nb % bs == 0
    n_prompt_steps = b * s // t
    n_steps = n_prompt_steps + nb // bs
    assert x.shape == (n_steps * t, d)
    flat_spec = pl.BlockSpec((t, d), lambda i: (i, 0))

    def prompt_state(rows):
        return pl.BlockSpec(
            (1, rows, c), lambda i: (jnp.minimum(i, n_prompt_steps - 1) // (s // t), 0, 0))

    def sample_state(rows):
        return pl.BlockSpec((bs, rows, c), lambda i: (jnp.maximum(i - n_prompt_steps, 0), 0, 0))

    def state_shape(n, rows):
        return jax.ShapeDtypeStruct((n, rows, c), F32)

    def ext_scratch(lead, rows):
        return pltpu.VMEM(lead + (rows, c), F32)

    scratch = ([ext_scratch((), CONV_PAD + t)] + [ext_scratch((), POOL_PAD + t)] * 4
               + [ext_scratch((bs,), CONV_PAD + tb), ext_scratch((bs,), POOL_PAD + tb)])
    weight_bytes = (d * in_dim + 2 * c * d + c * c) * win.dtype.itemsize
    scratch_bytes = (5 * (POOL_PAD + t) + 2 * bs * (POOL_PAD + tb)) * c * 4
    vmem = weight_bytes + 12 * t * d * 4 + scratch_bytes
    assert vmem < V7X_VMEM_BYTES
    return pl.pallas_call(
        functools.partial(_mixer_kernel, n_prompt_steps=n_prompt_steps, steps_per_seq=s // t),
        grid=(n_steps,),
        in_specs=[flat_spec, sample_state(CONV_W - 1), sample_state(POOL_HIST),
                  _resident((1, d)), _resident((d, in_dim)), _resident((CONV_W, c)),
                  _resident((groups, c // groups, c // groups)), _resident((1, c)),
                  _resident((2 * c, d))],
        out_specs=[flat_spec, prompt_state(CONV_W - 1), prompt_state(POOL_HIST),
                   sample_state(CONV_W - 1), sample_state(POOL_HIST)],
        out_shape=[jax.ShapeDtypeStruct(x.shape, F32),
                   state_shape(b, CONV_W - 1), state_shape(b, POOL_HIST),
                   state_shape(nb, CONV_W - 1), state_shape(nb, POOL_HIST)],
        scratch_shapes=scratch,
        compiler_params=pltpu.CompilerParams(
            dimension_semantics=("arbitrary",), vmem_limit_bytes=vmem),
        name="mixer",
    )(x, conv_hist, pool_hist, gain, win, convw, poolw, pscale, wout)


def kernel(x_prompt, x_sample, state_conv, state_pool, norm_ffn1, ffn1_gate, ffn1_up, ffn1_down,
           norm_mix, w_in, conv_w, pool_w, pool_scale, w_out, norm_ffn2, ffn2_gate, ffn2_up,
           ffn2_down, norm_final):
    row = lambda v: v.reshape(1, -1)
    gain_final = row(norm_final)
    group_shapes = (x_prompt.shape, x_sample.shape)
    x = _ffn((x_prompt, x_sample), row(norm_ffn1), ffn1_gate, ffn1_up, ffn1_down, gain_final,
             group_shapes=group_shapes, split_in=True, split_out=False, final_norm=False,
             name="ffn1")
    x, cp, pp, cs, ps = _mixer(x, state_conv, state_pool, row(norm_mix), w_in, conv_w, pool_w,
                               row(pool_scale), w_out, group_shapes=group_shapes)
    yp, ys = _ffn(x, row(norm_ffn2), ffn2_gate, ffn2_up, ffn2_down, gain_final,
                  group_shapes=group_shapes, split_in=False, split_out=True, final_norm=True,
                  name="ffn2")
    return (yp, ys, cp, pp, cs, ps)
```

```python
import functools

import jax
import jax.numpy as jnp
from jax.experimental import pallas as pl
from jax.experimental.pallas import tpu as pltpu

EPS = 1e-6
PAST_LEN = 16384
CONV_W = 3
POOL_WINDOWS = (2, 4, 8, 16)
POOL_HIST = max(POOL_WINDOWS) - 1
SUBLANES = 8
POOL_PAD = 24
CONV_PAD = 8
MXU_COLS = 256
V7X_VMEM_BYTES = 64 * 1024 * 1024
FFN_TOKENS = 1024
STAGE_SLOTS = 2
FFN_CHUNK = 256
MIX_TOKENS = 512

BF16 = jnp.bfloat16
F32 = jnp.float32


def _rms(x, gain):
    inv = jax.lax.rsqrt(jnp.mean(x * x, axis=-1, keepdims=True) + EPS)
    return x * inv * gain


def _dot(a, b):
    return jax.lax.dot_general(a, b, (((a.ndim - 1,), (0,)), ((), ())), preferred_element_type=F32)


def _resident(shape):
    return pl.BlockSpec(shape, lambda *_: (0,) * len(shape), pipeline_mode=pl.Buffered(1))


def _group_specs(n_prompt_steps, steps_per_seq, seqs_per_step, t, d, sample_mode=None):
    def prompt_map(i):
        j = jnp.minimum(i, n_prompt_steps - 1)
        return (j // steps_per_seq, j % steps_per_seq, 0)

    def sample_map(i):
        return (jnp.maximum(i - n_prompt_steps, 0), 0, 0)

    return (pl.BlockSpec((1, t, d), prompt_map),
            pl.BlockSpec((seqs_per_step, SUBLANES, d), sample_map, pipeline_mode=sample_mode))


def _ffn_kernel(*refs, n_prompt_steps, split_in, split_out, final_norm):
    n_in = 2 if split_in else 1
    n_out = 2 if split_out else 1
    x_refs = refs[:n_in]
    gain_ref, wg_hbm, wu_hbm, wd_hbm, gain_out_ref = refs[n_in:n_in + 5]
    o_refs = refs[n_in + 5:n_in + 5 + n_out]
    wg_ref, wu_ref, wd_ref, wg_stage, wu_stage, wd_stage, sems = refs[n_in + 5 + n_out:]
    step = pl.program_id(0)
    on_prompt = step < n_prompt_steps
    d_ff = wg_ref.shape[1]
    chunks = [slice(c0, c0 + FFN_CHUNK) for c0 in range(0, d_ff, FFN_CHUNK)]

    def weight_copies(c):
        cols, slot = chunks[c], c % STAGE_SLOTS
        return (pltpu.make_async_copy(wg_hbm.at[:, cols], wg_stage.at[slot], sems.at[0, slot]),
                pltpu.make_async_copy(wu_hbm.at[:, cols], wu_stage.at[slot], sems.at[1, slot]),
                pltpu.make_async_copy(wd_hbm.at[cols, :], wd_stage.at[slot], sems.at[2, slot]))

    def start_weight_chunk(c):
        for copy in weight_copies(c):
            copy.start()

    def land_weight_chunk(c):
        cols, slot = chunks[c], c % STAGE_SLOTS
        for copy in weight_copies(c):
            copy.wait()
        wg_ref[:, cols] = wg_stage[slot].astype(BF16)
        wu_ref[:, cols] = wu_stage[slot].astype(BF16)
        wd_ref[cols, :] = wd_stage[slot].astype(BF16)
        if c + STAGE_SLOTS < len(chunks):
            start_weight_chunk(c + STAGE_SLOTS)

    def tile(first_step):
        if split_in:
            xa_ref, xb_ref = x_refs
            x = jnp.where(on_prompt, xa_ref[0], xb_ref[...].reshape(xa_ref.shape[1:]))
        else:
            x = x_refs[0][...]
        h = _rms(x, gain_ref[...]).astype(BF16)
        act = []
        for c, cols in enumerate(chunks):
            if first_step:
                land_weight_chunk(c)
            g = _dot(h, wg_ref[:, cols])
            u = _dot(h, wu_ref[:, cols])
            act.append((g * jax.nn.sigmoid(g) * u).astype(BF16))
        y = x + 0.5 * _dot(jnp.concatenate(act, axis=-1), wd_ref[...])
        if final_norm:
            y = _rms(y, gain_out_ref[...])
        if split_out:
            oa_ref, ob_ref = o_refs

            @pl.when(on_prompt)
            def _():
                oa_ref[0] = y

            @pl.when(jnp.logical_not(on_prompt))
            def _():
                ob_ref[...] = y.reshape(ob_ref.shape)
        else:
            o_refs[0][...] = y

    @pl.when(step == 0)
    def _():
        for c in range(STAGE_SLOTS):
            start_weight_chunk(c)
        tile(first_step=True)

    @pl.when(step > 0)
    def _():
        tile(first_step=False)


def _ffn(x, gain, wg, wu, wd, gain_out, *, group_shapes, split_in, split_out, final_norm, name):
    (b, s, d), (nb, tb, _) = group_shapes
    d_ff = wg.shape[1]
    t = FFN_TOKENS
    assert s % t == 0 and tb == SUBLANES and (nb * tb) % t == 0 and d_ff % FFN_CHUNK == 0
    assert FFN_CHUNK % MXU_COLS == 0
    n_prompt_steps = b * s // t
    n_steps = n_prompt_steps + nb * tb // t
    sample_mode = pl.Buffered(1) if nb * tb == t else None
    pair_specs = _group_specs(n_prompt_steps, s // t, t // tb, t, d, sample_mode)
    flat_spec = pl.BlockSpec((t, d), lambda i: (i, 0))
    pair_shapes = [jax.ShapeDtypeStruct(shape, F32) for shape in group_shapes]
    flat_shape = jax.ShapeDtypeStruct((n_steps * t, d), F32)
    in_hbm = pl.BlockSpec(memory_space=pl.ANY)
    weight_bytes = 3 * d * d_ff * 2 + 3 * STAGE_SLOTS * d * FFN_CHUNK * wg.dtype.itemsize
    vmem = weight_bytes + 9 * t * d * 4
    assert vmem < V7X_VMEM_BYTES
    return pl.pallas_call(
        functools.partial(_ffn_kernel, n_prompt_steps=n_prompt_steps, split_in=split_in,
                          split_out=split_out, final_norm=final_norm),
        grid=(n_steps,),
        in_specs=[*(pair_specs if split_in else (flat_spec,)),
                  _resident((1, d)), in_hbm, in_hbm, in_hbm, _resident((1, d))],
        out_specs=list(pair_specs) if split_out else flat_spec,
        out_shape=pair_shapes if split_out else flat_shape,
        scratch_shapes=[pltpu.VMEM((d, d_ff), BF16), pltpu.VMEM((d, d_ff), BF16),
                        pltpu.VMEM((d_ff, d), BF16),
                        pltpu.VMEM((STAGE_SLOTS, d, FFN_CHUNK), wg.dtype),
                        pltpu.VMEM((STAGE_SLOTS, d, FFN_CHUNK), wu.dtype),
                        pltpu.VMEM((STAGE_SLOTS, FFN_CHUNK, d), wd.dtype),
                        pltpu.SemaphoreType.DMA((3, STAGE_SLOTS))],
        compiler_params=pltpu.CompilerParams(
            dimension_semantics=("arbitrary",), vmem_limit_bytes=vmem),
        name=name,
    )(*(x if split_in else (x,)), gain, wg, wu, wd, gain_out)


def _rows(lead, start, size, lanes=slice(None)):
    return lead + (slice(start, start + size), lanes)


def _mix_rows(x, proj, conv_ext, pool_ext, pool_lvl, first_pos,
              convw_ref, poolw_ref, pscale_ref, wout_ref):
    c = conv_ext.shape[-1]
    t = x.shape[-2]
    lead = (slice(None),) * (x.ndim - 2)
    gb = proj[..., 0:c]
    gc = proj[..., c:2 * c]
    u = proj[..., 2 * c:3 * c]
    p = proj[..., 3 * c:4 * c]

    conv_ext[_rows(lead, CONV_PAD, t)] = gc * u
    conv = None
    for k in range(CONV_W):
        term = conv_ext[_rows(lead, CONV_PAD - (CONV_W - 1) + k, t)] * convw_ref[k:k + 1, :]
        conv = term if conv is None else conv + term
    y_conv = gb * conv

    pool_ext[_rows(lead, POOL_PAD, t)] = p
    gdim = c // len(POOL_WINDOWS)
    wsums = []
    if pool_lvl is None:
        for g, w in enumerate(POOL_WINDOWS):
            lanes = slice(g * gdim, (g + 1) * gdim)
            wsum = pool_ext[_rows(lead, POOL_PAD, t, lanes)]
            for j in range(1, w):
                wsum = wsum + pool_ext[_rows(lead, POOL_PAD - j, t, lanes)]
            wsums.append(wsum)
    else:
        span = POOL_PAD - SUBLANES + t
        src = pool_ext
        for g, w in enumerate(POOL_WINDOWS):
            lanes = slice(g * gdim, c)
            shift = w // 2
            if g < len(pool_lvl):
                dst = pool_lvl[g]
                dst[_rows(lead, SUBLANES, span, lanes)] = (
                    src[_rows(lead, SUBLANES, span, lanes)]
                    + src[_rows(lead, SUBLANES - shift, span, lanes)])
                wsums.append(dst[_rows(lead, POOL_PAD, t, slice(g * gdim, (g + 1) * gdim))])
                src = dst
            else:
                wsums.append(src[_rows(lead, POOL_PAD, t, lanes)]
                             + src[_rows(lead, POOL_PAD - shift, t, lanes)])

    pos = first_pos + jax.lax.broadcasted_iota(jnp.int32, (t, 1), 0)
    y_pool = []
    for g, w in enumerate(POOL_WINDOWS):
        lanes = slice(g * gdim, (g + 1) * gdim)
        cnt = jnp.minimum(pos + 1, w).astype(F32)
        dlt = (wsums[g] / cnt - p[..., lanes]).astype(BF16)
        y = _dot(dlt.reshape(-1, gdim), poolw_ref[g]).reshape(dlt.shape)
        y_pool.append(y * pscale_ref[:, lanes])
    mix = jnp.concatenate([y_conv] + y_pool, axis=-1).astype(BF16)
    out = _dot(mix.reshape(-1, 2 * c), wout_ref[...])
    return x + out.reshape(x.shape)


def _state_rows(ext, pad, t, n):
    lead = (slice(None),) * (len(ext.shape) - 2)
    return ext[_rows(lead, pad + t - n, n)]


def _mixer_kernel(x_ref, conv_hist_ref, pool_hist_ref, gain_ref, win_ref, convw_ref,
                  poolw_ref, pscale_ref, wout_ref,
                  o_ref, conv_a_ref, pool_a_ref, conv_b_ref, pool_b_ref,
                  conv_ext_a, pool_ext_a, lvl1_a, lvl2_a, lvl3_a, conv_ext_b, pool_ext_b,
                  *, n_prompt_steps, steps_per_seq):
    i = pl.program_id(0)
    t, d = x_ref.shape
    weights = (convw_ref, poolw_ref, pscale_ref, wout_ref)

    @pl.when(i == 0)
    def _():
        for ref in (pool_ext_a, lvl1_a, lvl2_a):
            ref[0:POOL_PAD - POOL_HIST, :] = jnp.zeros((POOL_PAD - POOL_HIST, ref.shape[-1]), F32)

    @pl.when(i < n_prompt_steps)
    def _():
        ti = i % steps_per_seq

        @pl.when(ti == 0)
        def _():
            conv_ext_a[0:CONV_PAD, :] = jnp.zeros((CONV_PAD, conv_ext_a.shape[1]), F32)
            pool_ext_a[SUBLANES:POOL_PAD, :] = jnp.zeros(
                (POOL_PAD - SUBLANES, pool_ext_a.shape[1]), F32)

        @pl.when(ti > 0)
        def _():
            conv_ext_a[0:CONV_PAD, :] = conv_ext_a[t:t + CONV_PAD, :]
            pool_ext_a[SUBLANES:POOL_PAD, :] = pool_ext_a[t + SUBLANES:t + POOL_PAD, :]

        x = x_ref[...]
        h = _rms(x, gain_ref[...]).astype(BF16)
        proj = _dot(h, win_ref[...])
        o_ref[...] = _mix_rows(x, proj, conv_ext_a, pool_ext_a, (lvl1_a, lvl2_a, lvl3_a),
                               ti * t, *weights)
        conv_a_ref[0] = _state_rows(conv_ext_a, CONV_PAD, t, CONV_W - 1)
        pool_a_ref[0] = _state_rows(pool_ext_a, POOL_PAD, t, POOL_HIST)

    @pl.when(i >= n_prompt_steps)
    def _():
        bs, tb = conv_ext_b.shape[0], conv_ext_b.shape[1] - CONV_PAD
        conv_ext_b[:, CONV_PAD - (CONV_W - 1):CONV_PAD, :] = conv_hist_ref[...]
        pool_ext_b[:, POOL_PAD - POOL_HIST:POOL_PAD, :] = pool_hist_ref[...]
        x = x_ref[...]
        h = _rms(x, gain_ref[...]).astype(BF16)
        proj = _dot(h, win_ref[...])
        y = _mix_rows(x.reshape(bs, tb, d), proj.reshape(bs, tb, win_ref.shape[1]),
                      conv_ext_b, pool_ext_b, None, PAST_LEN, *weights)
        o_ref[...] = y.reshape(t, d)
        conv_b_ref[...] = _state_rows(conv_ext_b, CONV_PAD, tb, CONV_W - 1)
        pool_b_ref[...] = _state_rows(pool_ext_b, POOL_PAD, tb, POOL_HIST)


def _mixer(x, conv_hist, pool_hist, gain, win, convw, poolw, pscale, wout, *, group_shapes):
    (b, s, d), (nb, tb, _) = group_shapes
    in_dim = win.shape[1]
    c = in_dim // 4
    groups = len(POOL_WINDOWS)
    t = MIX_TOKENS
    bs = t // tb
    assert s % t == 0 and tb == SUBLANES and nb % bs == 0
    n_prompt_steps = b * s // t
    n_steps = n_prompt_steps + nb // bs
    assert x.shape == (n_steps * t, d)
    flat_spec = pl.BlockSpec((t, d), lambda i: (i, 0))

    def prompt_state(rows):
        return pl.BlockSpec(
            (1, rows, c), lambda i: (jnp.minimum(i, n_prompt_steps - 1) // (s // t), 0, 0))

    def sample_state(rows):
        return pl.BlockSpec((bs, rows, c), lambda i: (jnp.maximum(i - n_prompt_steps, 0), 0, 0))

    def state_shape(n, rows):
        return jax.ShapeDtypeStruct((n, rows, c), F32)

    def ext_scratch(lead, rows):
        return pltpu.VMEM(lead + (rows, c), F32)

    scratch = ([ext_scratch((), CONV_PAD + t)] + [ext_scratch((), POOL_PAD + t)] * 4
               + [ext_scratch((bs,), CONV_PAD + tb), ext_scratch((bs,), POOL_PAD + tb)])
    weight_bytes = (d * in_dim + 2 * c * d + c * c) * win.dtype.itemsize
    scratch_bytes = (5 * (POOL_PAD + t) + 2 * bs * (POOL_PAD + tb)) * c * 4
    vmem = weight_bytes + 12 * t * d * 4 + scratch_bytes
    assert vmem < V7X_VMEM_BYTES
    return pl.pallas_call(
        functools.partial(_mixer_kernel, n_prompt_steps=n_prompt_steps, steps_per_seq=s // t),
        grid=(n_steps,),
        in_specs=[flat_spec, sample_state(CONV_W - 1), sample_state(POOL_HIST),
                  _resident((1, d)), _resident((d, in_dim)), _resident((CONV_W, c)),
                  _resident((groups, c // groups, c // groups)), _resident((1, c)),
                  _resident((2 * c, d))],
        out_specs=[flat_spec, prompt_state(CONV_W - 1), prompt_state(POOL_HIST),
                   sample_state(CONV_W - 1), sample_state(POOL_HIST)],
        out_shape=[jax.ShapeDtypeStruct(x.shape, F32),
                   state_shape(b, CONV_W - 1), state_shape(b, POOL_HIST),
                   state_shape(nb, CONV_W - 1), state_shape(nb, POOL_HIST)],
        scratch_shapes=scratch,
        compiler_params=pltpu.CompilerParams(
            dimension_semantics=("arbitrary",), vmem_limit_bytes=vmem),
        name="mixer",
    )(x, conv_hist, pool_hist, gain, win, convw, poolw, pscale, wout)


def kernel(x_prompt, x_sample, state_conv, state_pool, norm_ffn1, ffn1_gate, ffn1_up, ffn1_down,
           norm_mix, w_in, conv_w, pool_w, pool_scale, w_out, norm_ffn2, ffn2_gate, ffn2_up,
           ffn2_down, norm_final):
    row = lambda v: v.reshape(1, -1)
    gain_final = row(norm_final)
    group_shapes = (x_prompt.shape, x_sample.shape)
    x = _ffn((x_prompt, x_sample), row(norm_ffn1), ffn1_gate, ffn1_up, ffn1_down, gain_final,
             group_shapes=group_shapes, split_in=True, split_out=False, final_norm=False,
             name="ffn1")
    x, cp, pp, cs, ps = _mixer(x, state_conv, state_pool, row(norm_mix), w_in, conv_w, pool_w,
                               row(pool_scale), w_out, group_shapes=group_shapes)
    yp, ys = _ffn(x, row(norm_ffn2), ffn2_gate, ffn2_up, ffn2_down, gain_final,
                  group_shapes=group_shapes, split_in=False, split_out=True, final_norm=True,
                  name="ffn2")
    return (yp, ys, cp, pp, cs, ps)
```

```python
import functools

import jax
import jax.numpy as jnp
from jax.experimental import pallas as pl
from jax.experimental.pallas import tpu as pltpu

EPS = 1e-6
PAST_LEN = 16384
CONV_W = 3
POOL_WINDOWS = (2, 4, 8, 16)
POOL_HIST = max(POOL_WINDOWS) - 1
SUBLANES = 8
POOL_PAD = 24
CONV_PAD = 8
MXU_COLS = 256
V7X_VMEM_BYTES = 64 * 1024 * 1024
FFN_TOKENS = 512
STAGE_SLOTS = 2
FFN_CHUNK = 256
MIX_TOKENS = 512

BF16 = jnp.bfloat16
F32 = jnp.float32


def _rms(x, gain):
    inv = jax.lax.rsqrt(jnp.mean(x * x, axis=-1, keepdims=True) + EPS)
    return x * inv * gain


def _dot(a, b):
    return jax.lax.dot_general(a, b, (((a.ndim - 1,), (0,)), ((), ())), preferred_element_type=F32)


def _resident(shape):
    return pl.BlockSpec(shape, lambda *_: (0,) * len(shape), pipeline_mode=pl.Buffered(1))


def _group_specs(n_prompt_steps, steps_per_seq, seqs_per_step, t, d, sample_mode=None):
    def prompt_map(i):
        j = jnp.minimum(i, n_prompt_steps - 1)
        return (j // steps_per_seq, j % steps_per_seq, 0)

    def sample_map(i):
        return (jnp.maximum(i - n_prompt_steps, 0), 0, 0)

    return (pl.BlockSpec((1, t, d), prompt_map),
            pl.BlockSpec((seqs_per_step, SUBLANES, d), sample_map, pipeline_mode=sample_mode))


def _ffn_kernel(*refs, n_prompt_steps, split_in, split_out, final_norm):
    n_in = 2 if split_in else 1
    n_out = 2 if split_out else 1
    x_refs = refs[:n_in]
    gain_ref, wg_hbm, wu_hbm, wd_hbm, gain_out_ref = refs[n_in:n_in + 5]
    o_refs = refs[n_in + 5:n_in + 5 + n_out]
    wg_ref, wu_ref, wd_ref, wg_stage, wu_stage, wd_stage, sems = refs[n_in + 5 + n_out:]
    step = pl.program_id(0)
    on_prompt = step < n_prompt_steps
    d_ff = wg_ref.shape[1]
    chunks = [slice(c0, c0 + FFN_CHUNK) for c0 in range(0, d_ff, FFN_CHUNK)]

    def weight_copies(c):
        cols, slot = chunks[c], c % STAGE_SLOTS
        return (pltpu.make_async_copy(wg_hbm.at[:, cols], wg_stage.at[slot], sems.at[0, slot]),
                pltpu.make_async_copy(wu_hbm.at[:, cols], wu_stage.at[slot], sems.at[1, slot]),
                pltpu.make_async_copy(wd_hbm.at[cols, :], wd_stage.at[slot], sems.at[2, slot]))

    def start_weight_chunk(c):
        for copy in weight_copies(c):
            copy.start()

    def land_weight_chunk(c):
        cols, slot = chunks[c], c % STAGE_SLOTS
        for copy in weight_copies(c):
            copy.wait()
        wg_ref[:, cols] = wg_stage[slot].astype(BF16)
        wu_ref[:, cols] = wu_stage[slot].astype(BF16)
        wd_ref[cols, :] = wd_stage[slot].astype(BF16)
        if c + STAGE_SLOTS < len(chunks):
            start_weight_chunk(c + STAGE_SLOTS)

    def tile(first_step):
        if split_in:
            xa_ref, xb_ref = x_refs
            x = jnp.where(on_prompt, xa_ref[0], xb_ref[...].reshape(xa_ref.shape[1:]))
        else:
            x = x_refs[0][...]
        h = _rms(x, gain_ref[...]).astype(BF16)
        act = []
        for c, cols in enumerate(chunks):
            if first_step:
                land_weight_chunk(c)
            g = _dot(h, wg_ref[:, cols])
            u = _dot(h, wu_ref[:, cols])
            act.append((g * jax.nn.sigmoid(g) * u).astype(BF16))
        y = x + 0.5 * _dot(jnp.concatenate(act, axis=-1), wd_ref[...])
        if final_norm:
            y = _rms(y, gain_out_ref[...])
        if split_out:
            oa_ref, ob_ref = o_refs

            @pl.when(on_prompt)
            def _():
                oa_ref[0] = y

            @pl.when(jnp.logical_not(on_prompt))
            def _():
                ob_ref[...] = y.reshape(ob_ref.shape)
        else:
            o_refs[0][...] = y

    @pl.when(step == 0)
    def _():
        for c in range(STAGE_SLOTS):
            start_weight_chunk(c)
        tile(first_step=True)

    @pl.when(step > 0)
    def _():
        tile(first_step=False)


def _ffn(x, gain, wg, wu, wd, gain_out, *, group_shapes, split_in, split_out, final_norm, name):
    (b, s, d), (nb, tb, _) = group_shapes
    d_ff = wg.shape[1]
    t = FFN_TOKENS
    assert s % t == 0 and tb == SUBLANES and (nb * tb) % t == 0 and d_ff % FFN_CHUNK == 0
    assert FFN_CHUNK % MXU_COLS == 0
    n_prompt_steps = b * s // t
    n_steps = n_prompt_steps + nb * tb // t
    sample_mode = pl.Buffered(1) if nb * tb == t else None
    pair_specs = _group_specs(n_prompt_steps, s // t, t // tb, t, d, sample_mode)
    flat_spec = pl.BlockSpec((t, d), lambda i: (i, 0))
    pair_shapes = [jax.ShapeDtypeStruct(shape, F32) for shape in group_shapes]
    flat_shape = jax.ShapeDtypeStruct((n_steps * t, d), F32)
    in_hbm = pl.BlockSpec(memory_space=pl.ANY)
    weight_bytes = 3 * d * d_ff * 2 + 3 * STAGE_SLOTS * d * FFN_CHUNK * wg.dtype.itemsize
    vmem = weight_bytes + 9 * t * d * 4
    assert vmem < V7X_VMEM_BYTES
    return pl.pallas_call(
        functools.partial(_ffn_kernel, n_prompt_steps=n_prompt_steps, split_in=split_in,
                          split_out=split_out, final_norm=final_norm),
        grid=(n_steps,),
        in_specs=[*(pair_specs if split_in else (flat_spec,)),
                  _resident((1, d)), in_hbm, in_hbm, in_hbm, _resident((1, d))],
        out_specs=list(pair_specs) if split_out else flat_spec,
        out_shape=pair_shapes if split_out else flat_shape,
        scratch_shapes=[pltpu.VMEM((d, d_ff), BF16), pltpu.VMEM((d, d_ff), BF16),
                        pltpu.VMEM((d_ff, d), BF16),
                        pltpu.VMEM((STAGE_SLOTS, d, FFN_CHUNK), wg.dtype),
                        pltpu.VMEM((STAGE_SLOTS, d, FFN_CHUNK), wu.dtype),
                        pltpu.VMEM((STAGE_SLOTS, FFN_CHUNK, d), wd.dtype),
                        pltpu.SemaphoreType.DMA((3, STAGE_SLOTS))],
        compiler_params=pltpu.CompilerParams(
            dimension_semantics=("arbitrary",), vmem_limit_bytes=vmem),
        name=name,
    )(*(x if split_in else (x,)), gain, wg, wu, wd, gain_out)


def _rows(lead, start, size, lanes=slice(None)):
    return lead + (slice(start, start + size), lanes)


def _mix_rows(x, proj, conv_ext, pool_ext, pool_lvl, first_pos,
              convw_ref, poolw_ref, pscale_ref, wout_ref):
    c = conv_ext.shape[-1]
    t = x.shape[-2]
    lead = (slice(None),) * (x.ndim - 2)
    gb = proj[..., 0:c]
    gc = proj[..., c:2 * c]
    u = proj[..., 2 * c:3 * c]
    p = proj[..., 3 * c:4 * c]

    conv_ext[_rows(lead, CONV_PAD, t)] = gc * u
    conv = None
    for k in range(CONV_W):
        term = conv_ext[_rows(lead, CONV_PAD - (CONV_W - 1) + k, t)] * convw_ref[k:k + 1, :]
        conv = term if conv is None else conv + term
    y_conv = gb * conv

    pool_ext[_rows(lead, POOL_PAD, t)] = p
    gdim = c // len(POOL_WINDOWS)
    wsums = []
    if pool_lvl is None:
        for g, w in enumerate(POOL_WINDOWS):
            lanes = slice(g * gdim, (g + 1) * gdim)
            wsum = pool_ext[_rows(lead, POOL_PAD, t, lanes)]
            for j in range(1, w):
                wsum = wsum + pool_ext[_rows(lead, POOL_PAD - j, t, lanes)]
            wsums.append(wsum)
    else:
        span = POOL_PAD - SUBLANES + t
        src = pool_ext
        for g, w in enumerate(POOL_WINDOWS):
            lanes = slice(g * gdim, c)
            shift = w // 2
            if g < len(pool_lvl):
                dst = pool_lvl[g]
                dst[_rows(lead, SUBLANES, span, lanes)] = (
                    src[_rows(lead, SUBLANES, span, lanes)]
                    + src[_rows(lead, SUBLANES - shift, span, lanes)])
                wsums.append(dst[_rows(lead, POOL_PAD, t, slice(g * gdim, (g + 1) * gdim))])
                src = dst
            else:
                wsums.append(src[_rows(lead, POOL_PAD, t, lanes)]
                             + src[_rows(lead, POOL_PAD - shift, t, lanes)])

    pos = first_pos + jax.lax.broadcasted_iota(jnp.int32, (t, 1), 0)
    y_pool = []
    for g, w in enumerate(POOL_WINDOWS):
        lanes = slice(g * gdim, (g + 1) * gdim)
        cnt = jnp.minimum(pos + 1, w).astype(F32)
        dlt = (wsums[g] / cnt - p[..., lanes]).astype(BF16)
        y = _dot(dlt.reshape(-1, gdim), poolw_ref[g]).reshape(dlt.shape)
        y_pool.append(y * pscale_ref[:, lanes])
    mix = jnp.concatenate([y_conv] + y_pool, axis=-1).astype(BF16)
    out = _dot(mix.reshape(-1, 2 * c), wout_ref[...])
    return x + out.reshape(x.shape)


def _state_rows(ext, pad, t, n):
    lead = (slice(None),) * (len(ext.shape) - 2)
    return ext[_rows(lead, pad + t - n, n)]


def _mixer_kernel(x_ref, conv_hist_ref, pool_hist_ref, gain_ref, win_ref, convw_ref,
                  poolw_ref, pscale_ref, wout_ref,
                  o_ref, conv_a_ref, pool_a_ref, conv_b_ref, pool_b_ref,
                  conv_ext_a, pool_ext_a, lvl1_a, lvl2_a, lvl3_a, conv_ext_b, pool_ext_b,
                  *, n_prompt_steps, steps_per_seq):
    i = pl.program_id(0)
    t, d = x_ref.shape
    weights = (convw_ref, poolw_ref, pscale_ref, wout_ref)

    @pl.when(i == 0)
    def _():
        for ref in (pool_ext_a, lvl1_a, lvl2_a):
            ref[0:POOL_PAD - POOL_HIST, :] = jnp.zeros((POOL_PAD - POOL_HIST, ref.shape[-1]), F32)

    @pl.when(i < n_prompt_steps)
    def _():
        ti = i % steps_per_seq

        @pl.when(ti == 0)
        def _():
            conv_ext_a[0:CONV_PAD, :] = jnp.zeros((CONV_PAD, conv_ext_a.shape[1]), F32)
            pool_ext_a[SUBLANES:POOL_PAD, :] = jnp.zeros(
                (POOL_PAD - SUBLANES, pool_ext_a.shape[1]), F32)

        @pl.when(ti > 0)
        def _():
            conv_ext_a[0:CONV_PAD, :] = conv_ext_a[t:t + CONV_PAD, :]
            pool_ext_a[SUBLANES:POOL_PAD, :] = pool_ext_a[t + SUBLANES:t + POOL_PAD, :]

        x = x_ref[...]
        h = _rms(x, gain_ref[...]).astype(BF16)
        proj = _dot(h, win_ref[...])
        o_ref[...] = _mix_rows(x, proj, conv_ext_a, pool_ext_a, (lvl1_a, lvl2_a, lvl3_a),
                               ti * t, *weights)
        conv_a_ref[0] = _state_rows(conv_ext_a, CONV_PAD, t, CONV_W - 1)
        pool_a_ref[0] = _state_rows(pool_ext_a, POOL_PAD, t, POOL_HIST)

    @pl.when(i >= n_prompt_steps)
    def _():
        bs, tb = conv_ext_b.shape[0], conv_ext_b.shape[1] - CONV_PAD
        conv_ext_b[:, CONV_PAD - (CONV_W - 1):CONV_PAD, :] = conv_hist_ref[...]
        pool_ext_b[:, POOL_PAD - POOL_HIST:POOL_PAD, :] = pool_hist_ref[...]
        x = x_ref[...]
        h = _rms(x, gain_ref[...]).astype(BF16)
        proj = _dot(h, win_ref[...])
        y = _mix_rows(x.reshape(bs, tb, d), proj.reshape(bs, tb, win_ref.shape[1]),
                      conv_ext_b, pool_ext_b, None, PAST_LEN, *weights)
        o_ref[...] = y.reshape(t, d)
        conv_b_ref[...] = _state_rows(conv_ext_b, CONV_PAD, tb, CONV_W - 1)
        pool_b_ref[...] = _state_rows(pool_ext_b, POOL_PAD, tb, POOL_HIST)


def _mixer(x, conv_hist, pool_hist, gain, win, convw, poolw, pscale, wout, *, group_shapes):
    (b, s, d), (nb, tb, _) = group_shapes
    in_dim = win.shape[1]
    c = in_dim // 4
    groups = len(POOL_WINDOWS)
    t = MIX_TOKENS
    bs = t // tb
    assert s % t == 0 and tb == SUBLANES and nb % bs == 0
    n_prompt_steps = b * s // t
    n_steps = n_prompt_steps + nb // bs
    assert x.shape == (n_steps * t, d)
    flat_spec = pl.BlockSpec((t, d), lambda i: (i, 0))

    def prompt_state(rows):
        return pl.BlockSpec(
            (1, rows, c), lambda i: (jnp.minimum(i, n_prompt_steps - 1) // (s // t), 0, 0))

    def sample_state(rows):
        return pl.BlockSpec((bs, rows, c), lambda i: (jnp.maximum(i - n_prompt_steps, 0), 0, 0))

    def state_shape(n, rows):
        return jax.ShapeDtypeStruct((n, rows, c), F32)

    def ext_scratch(lead, rows):
        return pltpu.VMEM(lead + (rows, c), F32)

    scratch = ([ext_scratch((), CONV_PAD + t)] + [ext_scratch((), POOL_PAD + t)] * 4
               + [ext_scratch((bs,), CONV_PAD + tb), ext_scratch((bs,), POOL_PAD + tb)])
    weight_bytes = (d * in_dim + 2 * c * d + c * c) * win.dtype.itemsize
    scratch_bytes = (5 * (POOL_PAD + t) + 2 * bs * (POOL_PAD + tb)) * c * 4
    vmem = weight_bytes + 12 * t * d * 4 + scratch_bytes
    assert vmem < V7X_VMEM_BYTES
    return pl.pallas_call(
        functools.partial(_mixer_kernel, n_prompt_steps=n_prompt_steps, steps_per_seq=s // t),
        grid=(n_steps,),
        in_specs=[flat_spec, sample_state(CONV_W - 1), sample_state(POOL_HIST),
                  _resident((1, d)), _resident((d, in_dim)), _resident((CONV_W, c)),
                  _resident((groups, c // groups, c // groups)), _resident((1, c)),
                  _resident((2 * c, d))],
        out_specs=[flat_spec, prompt_state(CONV_W - 1), prompt_state(POOL_HIST),
                   sample_state(CONV_W - 1), sample_state(POOL_HIST)],
        out_shape=[jax.ShapeDtypeStruct(x.shape, F32),
                   state_shape(b, CONV_W - 1), state_shape(b, POOL_HIST),
                   state_shape(nb, CONV_W - 1), state_shape(nb, POOL_HIST)],
        scratch_shapes=scratch,
        compiler_params=pltpu.CompilerParams(
            dimension_semantics=("arbitrary",), vmem_limit_bytes=vmem),
        name="mixer",
    )(x, conv_hist, pool_hist, gain, win, convw, poolw, pscale, wout)


def kernel(x_prompt, x_sample, state_conv, state_pool, norm_ffn1, ffn1_gate, ffn1_up, ffn1_down,
           norm_mix, w_in, conv_w, pool_w, pool_scale, w_out, norm_ffn2, ffn2_gate, ffn2_up,
           ffn2_down, norm_final):
    row = lambda v: v.reshape(1, -1)
    gain_final = row(norm_final)
    group_shapes = (x_prompt.shape, x_sample.shape)
    x = _ffn((x_prompt, x_sample), row(norm_ffn1), ffn1_gate, ffn1_up, ffn1_down, gain_final,
             group_shapes=group_shapes, split_in=True, split_out=False, final_norm=False,
             name="ffn1")
    x, cp, pp, cs, ps = _mixer(x, state_conv, state_pool, row(norm_mix), w_in, conv_w, pool_w,
                               row(pool_scale), w_out, group_shapes=group_shapes)
    yp, ys = _ffn(x, row(norm_ffn2), ffn2_gate, ffn2_up, ffn2_down, gain_final,
                  group_shapes=group_shapes, split_in=False, split_out=True, final_norm=True,
                  name="ffn2")
    return (yp, ys, cp, pp, cs, ps)
```

```python
import functools

import jax
import jax.numpy as jnp
from jax.experimental import pallas as pl
from jax.experimental.pallas import tpu as pltpu

EPS = 1e-6
PAST_LEN = 16384
CONV_W = 3
POOL_WINDOWS = (2, 4, 8, 16)
POOL_HIST = max(POOL_WINDOWS) - 1
SUBLANES = 8
POOL_PAD = 24
CONV_PAD = 8
MXU_COLS = 256
V7X_VMEM_BYTES = 64 * 1024 * 1024
FFN_TOKENS = 512
STAGE_SLOTS = 2
FFN_CHUNK = 256
MIX_TOKENS = 512

BF16 = jnp.bfloat16
F32 = jnp.float32


def _rms(x, gain):
    inv = jax.lax.rsqrt(jnp.mean(x * x, axis=-1, keepdims=True) + EPS)
    return x * inv * gain


def _dot(a, b):
    return jax.lax.dot_general(a, b, (((a.ndim - 1,), (0,)), ((), ())), preferred_element_type=F32)


def _resident(shape):
    return pl.BlockSpec(shape, lambda *_: (0,) * len(shape), pipeline_mode=pl.Buffered(1))


def _group_specs(n_prompt_steps, steps_per_seq, seqs_per_step, t, d, sample_mode=None):
    def prompt_map(i):
        j = jnp.minimum(i, n_prompt_steps - 1)
        return (j // steps_per_seq, j % steps_per_seq, 0)

    def sample_map(i):
        return (jnp.maximum(i - n_prompt_steps, 0), 0, 0)

    return (pl.BlockSpec((1, t, d), prompt_map),
            pl.BlockSpec((seqs_per_step, SUBLANES, d), sample_map, pipeline_mode=sample_mode))


def _ffn_kernel(*refs, n_prompt_steps, split_in, split_out, final_norm):
    n_in = 2 if split_in else 1
    n_out = 2 if split_out else 1
    x_refs = refs[:n_in]
    gain_ref, wg_hbm, wu_hbm, wd_hbm, gain_out_ref = refs[n_in:n_in + 5]
    o_refs = refs[n_in + 5:n_in + 5 + n_out]
    wg_ref, wu_ref, wd_ref, wg_stage, wu_stage, wd_stage, sems = refs[n_in + 5 + n_out:]
    step = pl.program_id(0)
    on_prompt = step < n_prompt_steps
    d_ff = wg_ref.shape[1]
    chunks = [slice(c0, c0 + FFN_CHUNK) for c0 in range(0, d_ff, FFN_CHUNK)]

    def weight_copies(c):
        cols, slot = chunks[c], c % STAGE_SLOTS
        return (pltpu.make_async_copy(wg_hbm.at[:, cols], wg_stage.at[slot], sems.at[0, slot]),
                pltpu.make_async_copy(wu_hbm.at[:, cols], wu_stage.at[slot], sems.at[1, slot]),
                pltpu.make_async_copy(wd_hbm.at[cols, :], wd_stage.at[slot], sems.at[2, slot]))

    def start_weight_chunk(c):
        for copy in weight_copies(c):
            copy.start()

    def land_weight_chunk(c):
        cols, slot = chunks[c], c % STAGE_SLOTS
        for copy in weight_copies(c):
            copy.wait()
        wg_ref[:, cols] = wg_stage[slot].astype(BF16)
        wu_ref[:, cols] = wu_stage[slot].astype(BF16)
        wd_ref[cols, :] = wd_stage[slot].astype(BF16)
        if c + STAGE_SLOTS < len(chunks):
            start_weight_chunk(c + STAGE_SLOTS)

    def tile(first_step):
        if split_in:
            xa_ref, xb_ref = x_refs
            x = jnp.where(on_prompt, xa_ref[0], xb_ref[...].reshape(xa_ref.shape[1:]))
        else:
            x = x_refs[0][...]
        h = _rms(x, gain_ref[...]).astype(BF16)
        act = []
        for c, cols in enumerate(chunks):
            if first_step:
                land_weight_chunk(c)
            g = _dot(h, wg_ref[:, cols])
            u = _dot(h, wu_ref[:, cols])
            act.append((g * jax.nn.sigmoid(g) * u).astype(BF16))
        y = x + 0.5 * _dot(jnp.concatenate(act, axis=-1), wd_ref[...])
        if final_norm:
            y = _rms(y, gain_out_ref[...])
        if split_out:
            oa_ref, ob_ref = o_refs

            @pl.when(on_prompt)
            def _():
                oa_ref[0] = y

            @pl.when(jnp.logical_not(on_prompt))
            def _():
                ob_ref[...] = y.reshape(ob_ref.shape)
        else:
            o_refs[0][...] = y

    @pl.when(step == 0)
    def _():
        for c in range(STAGE_SLOTS):
            start_weight_chunk(c)
        tile(first_step=True)

    @pl.when(step > 0)
    def _():
        tile(first_step=False)


def _ffn(x, gain, wg, wu, wd, gain_out, *, group_shapes, split_in, split_out, final_norm, name):
    (b, s, d), (nb, tb, _) = group_shapes
    d_ff = wg.shape[1]
    t = FFN_TOKENS
    assert s % t == 0 and tb == SUBLANES and (nb * tb) % t == 0 and d_ff % FFN_CHUNK == 0
    assert FFN_CHUNK % MXU_COLS == 0
    n_prompt_steps = b * s // t
    n_steps = n_prompt_steps + nb * tb // t
    sample_mode = pl.Buffered(1) if nb * tb == t else None
    pair_specs = _group_specs(n_prompt_steps, s // t, t // tb, t, d, sample_mode)
    flat_spec = pl.BlockSpec((t, d), lambda i: (i, 0))
    pair_shapes = [jax.ShapeDtypeStruct(shape, F32) for shape in group_shapes]
    flat_shape = jax.ShapeDtypeStruct((n_steps * t, d), F32)
    in_hbm = pl.BlockSpec(memory_space=pl.ANY)
    weight_bytes = 3 * d * d_ff * 2 + 3 * STAGE_SLOTS * d * FFN_CHUNK * wg.dtype.itemsize
    vmem = weight_bytes + 9 * t * d * 4
    assert vmem < V7X_VMEM_BYTES
    return pl.pallas_call(
        functools.partial(_ffn_kernel, n_prompt_steps=n_prompt_steps, split_in=split_in,
                          split_out=split_out, final_norm=final_norm),
        grid=(n_steps,),
        in_specs=[*(pair_specs if split_in else (flat_spec,)),
                  _resident((1, d)), in_hbm, in_hbm, in_hbm, _resident((1, d))],
        out_specs=list(pair_specs) if split_out else flat_spec,
        out_shape=pair_shapes if split_out else flat_shape,
        scratch_shapes=[pltpu.VMEM((d, d_ff), BF16), pltpu.VMEM((d, d_ff), BF16),
                        pltpu.VMEM((d_ff, d), BF16),
                        pltpu.VMEM((STAGE_SLOTS, d, FFN_CHUNK), wg.dtype),
                        pltpu.VMEM((STAGE_SLOTS, d, FFN_CHUNK), wu.dtype),
                        pltpu.VMEM((STAGE_SLOTS, FFN_CHUNK, d), wd.dtype),
                        pltpu.SemaphoreType.DMA((3, STAGE_SLOTS))],
        compiler_params=pltpu.CompilerParams(
            dimension_semantics=("arbitrary",), vmem_limit_bytes=vmem),
        name=name,
    )(*(x if split_in else (x,)), gain, wg, wu, wd, gain_out)


def _rows(lead, start, size, lanes=slice(None)):
    return lead + (slice(start, start + size), lanes)


def _mix_rows(x, h, win_ref, conv_ext, pool_ext, pool_lvl, first_pos,
              convw_ref, poolw_ref, pscale_ref, wout_ref):
    c = conv_ext.shape[-1]
    t = x.shape[-2]
    lead = (slice(None),) * (x.ndim - 2)

    def project(k):
        return _dot(h, win_ref[:, k * c:(k + 1) * c]).reshape(x.shape[:-1] + (c,))

    p, gc, u, gb = project(3), project(1), project(2), project(0)
    pool_ext[_rows(lead, POOL_PAD, t)] = p
    gdim = c // len(POOL_WINDOWS)
    wsums = []
    if pool_lvl is None:
        for g, w in enumerate(POOL_WINDOWS):
            lanes = slice(g * gdim, (g + 1) * gdim)
            wsum = pool_ext[_rows(lead, POOL_PAD, t, lanes)]
            for j in range(1, w):
                wsum = wsum + pool_ext[_rows(lead, POOL_PAD - j, t, lanes)]
            wsums.append(wsum)
    else:
        span = POOL_PAD - SUBLANES + t
        src = pool_ext
        for g, w in enumerate(POOL_WINDOWS):
            lanes = slice(g * gdim, c)
            shift = w // 2
            if g < len(pool_lvl):
                dst = pool_lvl[g]
                dst[_rows(lead, SUBLANES, span, lanes)] = (
                    src[_rows(lead, SUBLANES, span, lanes)]
                    + src[_rows(lead, SUBLANES - shift, span, lanes)])
                wsums.append(dst[_rows(lead, POOL_PAD, t, slice(g * gdim, (g + 1) * gdim))])
                src = dst
            else:
                wsums.append(src[_rows(lead, POOL_PAD, t, lanes)]
                             + src[_rows(lead, POOL_PAD - shift, t, lanes)])

    pos = first_pos + jax.lax.broadcasted_iota(jnp.int32, (t, 1), 0)
    y_pool = []
    for g, w in enumerate(POOL_WINDOWS):
        lanes = slice(g * gdim, (g + 1) * gdim)
        cnt = jnp.minimum(pos + 1, w).astype(F32)
        dlt = (wsums[g] / cnt - p[..., lanes]).astype(BF16)
        y = _dot(dlt.reshape(-1, gdim), poolw_ref[g]).reshape(dlt.shape)
        y_pool.append(y * pscale_ref[:, lanes])

    conv_ext[_rows(lead, CONV_PAD, t)] = gc * u
    conv = None
    for k in range(CONV_W):
        term = conv_ext[_rows(lead, CONV_PAD - (CONV_W - 1) + k, t)] * convw_ref[k:k + 1, :]
        conv = term if conv is None else conv + term
    y_conv = gb * conv

    mix = jnp.concatenate([y_conv] + y_pool, axis=-1).astype(BF16)
    out = _dot(mix.reshape(-1, 2 * c), wout_ref[...])
    return x + out.reshape(x.shape)


def _state_rows(ext, pad, t, n):
    lead = (slice(None),) * (len(ext.shape) - 2)
    return ext[_rows(lead, pad + t - n, n)]


def _mixer_kernel(x_ref, conv_hist_ref, pool_hist_ref, gain_ref, win_ref, convw_ref,
                  poolw_ref, pscale_ref, wout_ref,
                  o_ref, conv_a_ref, pool_a_ref, conv_b_ref, pool_b_ref,
                  conv_ext_a, pool_ext_a, lvl1_a, lvl2_a, lvl3_a, conv_ext_b, pool_ext_b,
                  *, n_prompt_steps, steps_per_seq):
    i = pl.program_id(0)
    t, d = x_ref.shape
    weights = (convw_ref, poolw_ref, pscale_ref, wout_ref)

    @pl.when(i == 0)
    def _():
        for ref in (pool_ext_a, lvl1_a, lvl2_a):
            ref[0:POOL_PAD - POOL_HIST, :] = jnp.zeros((POOL_PAD - POOL_HIST, ref.shape[-1]), F32)

    @pl.when(i < n_prompt_steps)
    def _():
        ti = i % steps_per_seq

        @pl.when(ti == 0)
        def _():
            conv_ext_a[0:CONV_PAD, :] = jnp.zeros((CONV_PAD, conv_ext_a.shape[1]), F32)
            pool_ext_a[SUBLANES:POOL_PAD, :] = jnp.zeros(
                (POOL_PAD - SUBLANES, pool_ext_a.shape[1]), F32)

        @pl.when(ti > 0)
        def _():
            conv_ext_a[0:CONV_PAD, :] = conv_ext_a[t:t + CONV_PAD, :]
            pool_ext_a[SUBLANES:POOL_PAD, :] = pool_ext_a[t + SUBLANES:t + POOL_PAD, :]

        x = x_ref[...]
        h = _rms(x, gain_ref[...]).astype(BF16)
        o_ref[...] = _mix_rows(x, h, win_ref, conv_ext_a, pool_ext_a, (lvl1_a, lvl2_a, lvl3_a),
                               ti * t, *weights)
        conv_a_ref[0] = _state_rows(conv_ext_a, CONV_PAD, t, CONV_W - 1)
        pool_a_ref[0] = _state_rows(pool_ext_a, POOL_PAD, t, POOL_HIST)

    @pl.when(i >= n_prompt_steps)
    def _():
        bs, tb = conv_ext_b.shape[0], conv_ext_b.shape[1] - CONV_PAD
        conv_ext_b[:, CONV_PAD - (CONV_W - 1):CONV_PAD, :] = conv_hist_ref[...]
        pool_ext_b[:, POOL_PAD - POOL_HIST:POOL_PAD, :] = pool_hist_ref[...]
        x = x_ref[...]
        h = _rms(x, gain_ref[...]).astype(BF16)
        y = _mix_rows(x.reshape(bs, tb, d), h, win_ref, conv_ext_b, pool_ext_b, None, PAST_LEN,
                      *weights)
        o_ref[...] = y.reshape(t, d)
        conv_b_ref[...] = _state_rows(conv_ext_b, CONV_PAD, tb, CONV_W - 1)
        pool_b_ref[...] = _state_rows(pool_ext_b, POOL_PAD, tb, POOL_HIST)


def _mixer(x, conv_hist, pool_hist, gain, win, convw, poolw, pscale, wout, *, group_shapes):
    (b, s, d), (nb, tb, _) = group_shapes
    in_dim = win.shape[1]
    c = in_dim // 4
    groups = len(POOL_WINDOWS)
    t = MIX_TOKENS
    bs = t // tb
    assert s % t == 0 and tb == SUBLANES and nb % bs == 0
    n_prompt_steps = b * s // t
    n_steps = n_prompt_steps + nb // bs
    assert x.shape == (n_steps * t, d)
    flat_spec = pl.BlockSpec((t, d), lambda i: (i, 0))

    def prompt_state(rows):
        return pl.BlockSpec(
            (1, rows, c), lambda i: (jnp.minimum(i, n_prompt_steps - 1) // (s // t), 0, 0))

    def sample_state(rows):
        return pl.BlockSpec((bs, rows, c), lambda i: (jnp.maximum(i - n_prompt_steps, 0), 0, 0))

    def state_shape(n, rows):
        return jax.ShapeDtypeStruct((n, rows, c), F32)

    def ext_scratch(lead, rows):
        return pltpu.VMEM(lead + (rows, c), F32)

    scratch = ([ext_scratch((), CONV_PAD + t)] + [ext_scratch((), POOL_PAD + t)] * 4
               + [ext_scratch((bs,), CONV_PAD + tb), ext_scratch((bs,), POOL_PAD + tb)])
    weight_bytes = (d * in_dim + 2 * c * d + c * c) * win.dtype.itemsize
    scratch_bytes = (5 * (POOL_PAD + t) + 2 * bs * (POOL_PAD + tb)) * c * 4
    vmem = weight_bytes + 12 * t * d * 4 + scratch_bytes
    assert vmem < V7X_VMEM_BYTES
    return pl.pallas_call(
        functools.partial(_mixer_kernel, n_prompt_steps=n_prompt_steps, steps_per_seq=s // t),
        grid=(n_steps,),
        in_specs=[flat_spec, sample_state(CONV_W - 1), sample_state(POOL_HIST),
                  _resident((1, d)), _resident((d, in_dim)), _resident((CONV_W, c)),
                  _resident((groups, c // groups, c // groups)), _resident((1, c)),
                  _resident((2 * c, d))],
        out_specs=[flat_spec, prompt_state(CONV_W - 1), prompt_state(POOL_HIST),
                   sample_state(CONV_W - 1), sample_state(POOL_HIST)],
        out_shape=[jax.ShapeDtypeStruct(x.shape, F32),
                   state_shape(b, CONV_W - 1), state_shape(b, POOL_HIST),
                   state_shape(nb, CONV_W - 1), state_shape(nb, POOL_HIST)],
        scratch_shapes=scratch,
        compiler_params=pltpu.CompilerParams(
            dimension_semantics=("arbitrary",), vmem_limit_bytes=vmem),
        name="mixer",
    )(x, conv_hist, pool_hist, gain, win, convw, poolw, pscale, wout)


def kernel(x_prompt, x_sample, state_conv, state_pool, norm_ffn1, ffn1_gate, ffn1_up, ffn1_down,
           norm_mix, w_in, conv_w, pool_w, pool_scale, w_out, norm_ffn2, ffn2_gate, ffn2_up,
           ffn2_down, norm_final):
    row = lambda v: v.reshape(1, -1)
    gain_final = row(norm_final)
    group_shapes = (x_prompt.shape, x_sample.shape)
    x = _ffn((x_prompt, x_sample), row(norm_ffn1), ffn1_gate, ffn1_up, ffn1_down, gain_final,
             group_shapes=group_shapes, split_in=True, split_out=False, final_norm=False,
             name="ffn1")
    x, cp, pp, cs, ps = _mixer(x, state_conv, state_pool, row(norm_mix), w_in, conv_w, pool_w,
                               row(pool_scale), w_out, group_shapes=group_shapes)
    yp, ys = _ffn(x, row(norm_ffn2), ffn2_gate, ffn2_up, ffn2_down, gain_final,
                  group_shapes=group_shapes, split_in=False, split_out=True, final_norm=True,
                  name="ffn2")
    return (yp, ys, cp, pp, cs, ps)
```

```python
import functools

import jax
import jax.numpy as jnp
from jax.experimental import pallas as pl
from jax.experimental.pallas import tpu as pltpu

EPS = 1e-6
PAST_LEN = 16384
CONV_W = 3
POOL_WINDOWS = (2, 4, 8, 16)
POOL_HIST = max(POOL_WINDOWS) - 1
SUBLANES = 8
POOL_PAD = 24
CONV_PAD = 8
MXU_COLS = 256
V7X_VMEM_BYTES = 64 * 1024 * 1024
FFN_TOKENS = 512
STAGE_SLOTS = 2
FFN_CHUNK = 256
MIX_TOKENS = 512

BF16 = jnp.bfloat16
F32 = jnp.float32


def _rms(x, gain):
    inv = jax.lax.rsqrt(jnp.mean(x * x, axis=-1, keepdims=True) + EPS)
    return x * inv * gain


def _dot(a, b):
    return jax.lax.dot_general(a, b, (((a.ndim - 1,), (0,)), ((), ())), preferred_element_type=F32)


def _resident(shape):
    return pl.BlockSpec(shape, lambda *_: (0,) * len(shape), pipeline_mode=pl.Buffered(1))


def _group_specs(n_prompt_steps, steps_per_seq, seqs_per_step, t, d, sample_mode=None):
    def prompt_map(i):
        j = jnp.minimum(i, n_prompt_steps - 1)
        return (j // steps_per_seq, j % steps_per_seq, 0)

    def sample_map(i):
        return (jnp.maximum(i - n_prompt_steps, 0), 0, 0)

    return (pl.BlockSpec((1, t, d), prompt_map),
            pl.BlockSpec((seqs_per_step, SUBLANES, d), sample_map, pipeline_mode=sample_mode))


def _ffn_kernel(*refs, n_prompt_steps, split_in, split_out, final_norm):
    n_in = 2 if split_in else 1
    n_out = 2 if split_out else 1
    x_refs = refs[:n_in]
    gain_ref, wg_hbm, wu_hbm, wd_hbm, gain_out_ref = refs[n_in:n_in + 5]
    o_refs = refs[n_in + 5:n_in + 5 + n_out]
    wg_ref, wu_ref, wd_ref, wg_stage, wu_stage, wd_stage, sems = refs[n_in + 5 + n_out:]
    step = pl.program_id(0)
    on_prompt = step < n_prompt_steps
    d_ff = wg_ref.shape[1]
    chunks = [slice(c0, c0 + FFN_CHUNK) for c0 in range(0, d_ff, FFN_CHUNK)]

    def weight_copies(c):
        cols, slot = chunks[c], c % STAGE_SLOTS
        return (pltpu.make_async_copy(wg_hbm.at[:, cols], wg_stage.at[slot], sems.at[0, slot]),
                pltpu.make_async_copy(wu_hbm.at[:, cols], wu_stage.at[slot], sems.at[1, slot]),
                pltpu.make_async_copy(wd_hbm.at[cols, :], wd_stage.at[slot], sems.at[2, slot]))

    def start_weight_chunk(c):
        for copy in weight_copies(c):
            copy.start()

    def land_weight_chunk(c):
        cols, slot = chunks[c], c % STAGE_SLOTS
        for copy in weight_copies(c):
            copy.wait()
        wg_ref[:, cols] = wg_stage[slot].astype(BF16)
        wu_ref[:, cols] = wu_stage[slot].astype(BF16)
        wd_ref[cols, :] = wd_stage[slot].astype(BF16)
        if c + STAGE_SLOTS < len(chunks):
            start_weight_chunk(c + STAGE_SLOTS)

    def tile(first_step):
        if split_in:
            xa_ref, xb_ref = x_refs
            x = jnp.where(on_prompt, xa_ref[0], xb_ref[...].reshape(xa_ref.shape[1:]))
        else:
            x = x_refs[0][...]
        h = _rms(x, gain_ref[...]).astype(BF16)
        act = []
        for c, cols in enumerate(chunks):
            if first_step:
                land_weight_chunk(c)
            g = _dot(h, wg_ref[:, cols])
            u = _dot(h, wu_ref[:, cols])
            act.append((g * jax.nn.sigmoid(g) * u).astype(BF16))
        y = x + 0.5 * _dot(jnp.concatenate(act, axis=-1), wd_ref[...])
        if final_norm:
            y = _rms(y, gain_out_ref[...])
        if split_out:
            oa_ref, ob_ref = o_refs

            @pl.when(on_prompt)
            def _():
                oa_ref[0] = y

            @pl.when(jnp.logical_not(on_prompt))
            def _():
                ob_ref[...] = y.reshape(ob_ref.shape)
        else:
            o_refs[0][...] = y

    @pl.when(step == 0)
    def _():
        for c in range(STAGE_SLOTS):
            start_weight_chunk(c)
        tile(first_step=True)

    @pl.when(step > 0)
    def _():
        tile(first_step=False)


def _ffn(x, gain, wg, wu, wd, gain_out, *, group_shapes, split_in, split_out, final_norm, name):
    (b, s, d), (nb, tb, _) = group_shapes
    d_ff = wg.shape[1]
    t = FFN_TOKENS
    assert s % t == 0 and tb == SUBLANES and (nb * tb) % t == 0 and d_ff % FFN_CHUNK == 0
    assert FFN_CHUNK % MXU_COLS == 0
    n_prompt_steps = b * s // t
    n_steps = n_prompt_steps + nb * tb // t
    sample_mode = pl.Buffered(1) if nb * tb == t else None
    pair_specs = _group_specs(n_prompt_steps, s // t, t // tb, t, d, sample_mode)
    flat_spec = pl.BlockSpec((t, d), lambda i: (i, 0))
    pair_shapes = [jax.ShapeDtypeStruct(shape, F32) for shape in group_shapes]
    flat_shape = jax.ShapeDtypeStruct((n_steps * t, d), F32)
    in_hbm = pl.BlockSpec(memory_space=pl.ANY)
    weight_bytes = 3 * d * d_ff * 2 + 3 * STAGE_SLOTS * d * FFN_CHUNK * wg.dtype.itemsize
    vmem = weight_bytes + 9 * t * d * 4
    assert vmem < V7X_VMEM_BYTES
    return pl.pallas_call(
        functools.partial(_ffn_kernel, n_prompt_steps=n_prompt_steps, split_in=split_in,
                          split_out=split_out, final_norm=final_norm),
        grid=(n_steps,),
        in_specs=[*(pair_specs if split_in else (flat_spec,)),
                  _resident((1, d)), in_hbm, in_hbm, in_hbm, _resident((1, d))],
        out_specs=list(pair_specs) if split_out else flat_spec,
        out_shape=pair_shapes if split_out else flat_shape,
        scratch_shapes=[pltpu.VMEM((d, d_ff), BF16), pltpu.VMEM((d, d_ff), BF16),
                        pltpu.VMEM((d_ff, d), BF16),
                        pltpu.VMEM((STAGE_SLOTS, d, FFN_CHUNK), wg.dtype),
                        pltpu.VMEM((STAGE_SLOTS, d, FFN_CHUNK), wu.dtype),
                        pltpu.VMEM((STAGE_SLOTS, FFN_CHUNK, d), wd.dtype),
                        pltpu.SemaphoreType.DMA((3, STAGE_SLOTS))],
        compiler_params=pltpu.CompilerParams(
            dimension_semantics=("arbitrary",), vmem_limit_bytes=vmem),
        name=name,
    )(*(x if split_in else (x,)), gain, wg, wu, wd, gain_out)


def _rows(lead, start, size, lanes=slice(None)):
    return lead + (slice(start, start + size), lanes)


def _mix_rows(x, h, win_ref, conv_ext, pool_ext, pool_lvl, first_pos,
              convw_ref, poolw_ref, pscale_ref, wout_ref):
    c = conv_ext.shape[-1]
    t = x.shape[-2]
    lead = (slice(None),) * (x.ndim - 2)

    def project(k):
        return _dot(h, win_ref[:, k * c:(k + 1) * c]).reshape(x.shape[:-1] + (c,))

    p, gc, u, gb = project(3), project(1), project(2), project(0)
    pool_ext[_rows(lead, POOL_PAD, t)] = p
    gdim = c // len(POOL_WINDOWS)
    wsums = []
    if pool_lvl is None:
        for g, w in enumerate(POOL_WINDOWS):
            lanes = slice(g * gdim, (g + 1) * gdim)
            wsum = pool_ext[_rows(lead, POOL_PAD, t, lanes)]
            for j in range(1, w):
                wsum = wsum + pool_ext[_rows(lead, POOL_PAD - j, t, lanes)]
            wsums.append(wsum)
    else:
        span = POOL_PAD - SUBLANES + t
        src = pool_ext
        for g, w in enumerate(POOL_WINDOWS):
            lanes = slice(g * gdim, c)
            shift = w // 2
            if g < len(pool_lvl):
                dst = pool_lvl[g]
                dst[_rows(lead, SUBLANES, span, lanes)] = (
                    src[_rows(lead, SUBLANES, span, lanes)]
                    + src[_rows(lead, SUBLANES - shift, span, lanes)])
                wsums.append(dst[_rows(lead, POOL_PAD, t, slice(g * gdim, (g + 1) * gdim))])
                src = dst
            else:
                wsums.append(src[_rows(lead, POOL_PAD, t, lanes)]
                             + src[_rows(lead, POOL_PAD - shift, t, lanes)])

    pos = first_pos + jax.lax.broadcasted_iota(jnp.int32, (t, 1), 0)
    y_pool = []
    for g, w in enumerate(POOL_WINDOWS):
        lanes = slice(g * gdim, (g + 1) * gdim)
        cnt = jnp.minimum(pos + 1, w).astype(F32)
        dlt = (wsums[g] / cnt - p[..., lanes]).astype(BF16)
        y = _dot(dlt.reshape(-1, gdim), poolw_ref[g]).reshape(dlt.shape)
        y_pool.append(y * pscale_ref[:, lanes])

    conv_ext[_rows(lead, CONV_PAD, t)] = gc * u
    conv = None
    for k in range(CONV_W):
        term = conv_ext[_rows(lead, CONV_PAD - (CONV_W - 1) + k, t)] * convw_ref[k:k + 1, :]
        conv = term if conv is None else conv + term
    y_conv = gb * conv

    mix = jnp.concatenate([y_conv] + y_pool, axis=-1).astype(BF16)
    out = _dot(mix.reshape(-1, 2 * c), wout_ref[...])
    return x + out.reshape(x.shape)


def _state_rows(ext, pad, t, n):
    lead = (slice(None),) * (len(ext.shape) - 2)
    return ext[_rows(lead, pad + t - n, n)]


def _mixer_kernel(x_ref, conv_hist_ref, pool_hist_ref, gain_ref, win_hbm, convw_ref,
                  poolw_ref, pscale_ref, wout_hbm,
                  o_ref, conv_a_ref, pool_a_ref, conv_b_ref, pool_b_ref,
                  conv_ext_a, pool_ext_a, lvl1_a, lvl2_a, lvl3_a, conv_ext_b, pool_ext_b,
                  win_ref, wout_ref, stage, sems,
                  *, n_prompt_steps, steps_per_seq):
    i = pl.program_id(0)
    t, d = x_ref.shape
    weights = (convw_ref, poolw_ref, pscale_ref, wout_ref)

    def land_weights():
        width = stage.shape[2]
        parts = [(hbm, ref, slice(c0, c0 + width))
                 for hbm, ref in ((win_hbm, win_ref), (wout_hbm, wout_ref))
                 for c0 in range(0, ref.shape[1], width)]

        def copy(k):
            hbm, _, cols = parts[k]
            slot = k % STAGE_SLOTS
            return pltpu.make_async_copy(hbm.at[:, cols], stage.at[slot], sems.at[slot])

        for k in range(STAGE_SLOTS):
            copy(k).start()
        for k, (_, ref, cols) in enumerate(parts):
            copy(k).wait()
            ref[:, cols] = stage[k % STAGE_SLOTS].astype(BF16)
            if k + STAGE_SLOTS < len(parts):
                copy(k + STAGE_SLOTS).start()

    @pl.when(i == 0)
    def _():
        land_weights()
        for ref in (pool_ext_a, lvl1_a, lvl2_a):
            ref[0:POOL_PAD - POOL_HIST, :] = jnp.zeros((POOL_PAD - POOL_HIST, ref.shape[-1]), F32)

    @pl.when(i < n_prompt_steps)
    def _():
        ti = i % steps_per_seq

        @pl.when(ti == 0)
        def _():
            conv_ext_a[0:CONV_PAD, :] = jnp.zeros((CONV_PAD, conv_ext_a.shape[1]), F32)
            pool_ext_a[SUBLANES:POOL_PAD, :] = jnp.zeros(
                (POOL_PAD - SUBLANES, pool_ext_a.shape[1]), F32)

        @pl.when(ti > 0)
        def _():
            conv_ext_a[0:CONV_PAD, :] = conv_ext_a[t:t + CONV_PAD, :]
            pool_ext_a[SUBLANES:POOL_PAD, :] = pool_ext_a[t + SUBLANES:t + POOL_PAD, :]

        x = x_ref[...]
        h = _rms(x, gain_ref[...]).astype(BF16)
        o_ref[...] = _mix_rows(x, h, win_ref, conv_ext_a, pool_ext_a, (lvl1_a, lvl2_a, lvl3_a),
                               ti * t, *weights)
        conv_a_ref[0] = _state_rows(conv_ext_a, CONV_PAD, t, CONV_W - 1)
        pool_a_ref[0] = _state_rows(pool_ext_a, POOL_PAD, t, POOL_HIST)

    @pl.when(i >= n_prompt_steps)
    def _():
        bs, tb = conv_ext_b.shape[0], conv_ext_b.shape[1] - CONV_PAD
        conv_ext_b[:, CONV_PAD - (CONV_W - 1):CONV_PAD, :] = conv_hist_ref[...]
        pool_ext_b[:, POOL_PAD - POOL_HIST:POOL_PAD, :] = pool_hist_ref[...]
        x = x_ref[...]
        h = _rms(x, gain_ref[...]).astype(BF16)
        y = _mix_rows(x.reshape(bs, tb, d), h, win_ref, conv_ext_b, pool_ext_b, None, PAST_LEN,
                      *weights)
        o_ref[...] = y.reshape(t, d)
        conv_b_ref[...] = _state_rows(conv_ext_b, CONV_PAD, tb, CONV_W - 1)
        pool_b_ref[...] = _state_rows(pool_ext_b, POOL_PAD, tb, POOL_HIST)


def _mixer(x, conv_hist, pool_hist, gain, win, convw, poolw, pscale, wout, *, group_shapes):
    (b, s, d), (nb, tb, _) = group_shapes
    in_dim = win.shape[1]
    c = in_dim // 4
    groups = len(POOL_WINDOWS)
    t = MIX_TOKENS
    bs = t // tb
    assert s % t == 0 and tb == SUBLANES and nb % bs == 0
    n_prompt_steps = b * s // t
    n_steps = n_prompt_steps + nb // bs
    assert x.shape == (n_steps * t, d)
    flat_spec = pl.BlockSpec((t, d), lambda i: (i, 0))
    in_hbm = pl.BlockSpec(memory_space=pl.ANY)

    def prompt_state(rows):
        return pl.BlockSpec(
            (1, rows, c), lambda i: (jnp.minimum(i, n_prompt_steps - 1) // (s // t), 0, 0))

    def sample_state(rows):
        return pl.BlockSpec((bs, rows, c), lambda i: (jnp.maximum(i - n_prompt_steps, 0), 0, 0))

    def state_shape(n, rows):
        return jax.ShapeDtypeStruct((n, rows, c), F32)

    def ext_scratch(lead, rows):
        return pltpu.VMEM(lead + (rows, c), F32)

    scratch = ([ext_scratch((), CONV_PAD + t)] + [ext_scratch((), POOL_PAD + t)] * 4
               + [ext_scratch((bs,), CONV_PAD + tb), ext_scratch((bs,), POOL_PAD + tb)])
    assert wout.shape == (d, d) and in_dim % c == 0
    scratch += [pltpu.VMEM((d, in_dim), BF16), pltpu.VMEM(wout.shape, BF16),
                pltpu.VMEM((STAGE_SLOTS, d, c), win.dtype), pltpu.SemaphoreType.DMA((STAGE_SLOTS,))]
    weight_bytes = (d * in_dim + 2 * c * d) * 2 + STAGE_SLOTS * d * c * 4 + c * c * 4
    scratch_bytes = (5 * (POOL_PAD + t) + 2 * bs * (POOL_PAD + tb)) * c * 4
    vmem = weight_bytes + 12 * t * d * 4 + scratch_bytes
    assert vmem < V7X_VMEM_BYTES
    return pl.pallas_call(
        functools.partial(_mixer_kernel, n_prompt_steps=n_prompt_steps, steps_per_seq=s // t),
        grid=(n_steps,),
        in_specs=[flat_spec, sample_state(CONV_W - 1), sample_state(POOL_HIST),
                  _resident((1, d)), in_hbm, _resident((CONV_W, c)),
                  _resident((groups, c // groups, c // groups)), _resident((1, c)), in_hbm],
        out_specs=[flat_spec, prompt_state(CONV_W - 1), prompt_state(POOL_HIST),
                   sample_state(CONV_W - 1), sample_state(POOL_HIST)],
        out_shape=[jax.ShapeDtypeStruct(x.shape, F32),
                   state_shape(b, CONV_W - 1), state_shape(b, POOL_HIST),
                   state_shape(nb, CONV_W - 1), state_shape(nb, POOL_HIST)],
        scratch_shapes=scratch,
        compiler_params=pltpu.CompilerParams(
            dimension_semantics=("arbitrary",), vmem_limit_bytes=vmem),
        name="mixer",
    )(x, conv_hist, pool_hist, gain, win, convw, poolw, pscale, wout)


def kernel(x_prompt, x_sample, state_conv, state_pool, norm_ffn1, ffn1_gate, ffn1_up, ffn1_down,
           norm_mix, w_in, conv_w, pool_w, pool_scale, w_out, norm_ffn2, ffn2_gate, ffn2_up,
           ffn2_down, norm_final):
    row = lambda v: v.reshape(1, -1)
    gain_final = row(norm_final)
    group_shapes = (x_prompt.shape, x_sample.shape)
    x = _ffn((x_prompt, x_sample), row(norm_ffn1), ffn1_gate, ffn1_up, ffn1_down, gain_final,
             group_shapes=group_shapes, split_in=True, split_out=False, final_norm=False,
             name="ffn1")
    x, cp, pp, cs, ps = _mixer(x, state_conv, state_pool, row(norm_mix), w_in, conv_w, pool_w,
                               row(pool_scale), w_out, group_shapes=group_shapes)
    yp, ys = _ffn(x, row(norm_ffn2), ffn2_gate, ffn2_up, ffn2_down, gain_final,
                  group_shapes=group_shapes, split_in=False, split_out=True, final_norm=True,
                  name="ffn2")
    return (yp, ys, cp, pp, cs, ps)
```

```python
import functools

import jax
import jax.numpy as jnp
from jax.experimental import pallas as pl
from jax.experimental.pallas import tpu as pltpu

EPS = 1e-6
PAST_LEN = 16384
CONV_W = 3
POOL_WINDOWS = (2, 4, 8, 16)
POOL_HIST = max(POOL_WINDOWS) - 1
SUBLANES = 8
POOL_PAD = 24
CONV_PAD = 8
MXU_COLS = 256
V7X_VMEM_BYTES = 64 * 1024 * 1024
VMEM_CLAIM_BYTES = 60 * 1024 * 1024
FFN_TOKENS = 512
STAGE_SLOTS = 2
FFN_CHUNK = 256
MIX_TOKENS = 512

BF16 = jnp.bfloat16
F32 = jnp.float32


def _rms(x, gain):
    inv = jax.lax.rsqrt(jnp.mean(x * x, axis=-1, keepdims=True) + EPS)
    return x * inv * gain


def _dot(a, b):
    return jax.lax.dot_general(a, b, (((a.ndim - 1,), (0,)), ((), ())), preferred_element_type=F32)


def _resident(shape):
    return pl.BlockSpec(shape, lambda *_: (0,) * len(shape), pipeline_mode=pl.Buffered(1))


def _group_specs(n_prompt_steps, steps_per_seq, seqs_per_step, t, d, sample_mode=None):
    def prompt_map(i):
        j = jnp.minimum(i, n_prompt_steps - 1)
        return (j // steps_per_seq, j % steps_per_seq, 0)

    def sample_map(i):
        return (jnp.maximum(i - n_prompt_steps, 0), 0, 0)

    return (pl.BlockSpec((1, t, d), prompt_map),
            pl.BlockSpec((seqs_per_step, SUBLANES, d), sample_map, pipeline_mode=sample_mode))


def _ffn_kernel(*refs, n_prompt_steps, split_in, split_out, final_norm):
    n_in = 2 if split_in else 1
    n_out = 2 if split_out else 1
    x_refs = refs[:n_in]
    gain_ref, wg_hbm, wu_hbm, wd_hbm, gain_out_ref = refs[n_in:n_in + 5]
    o_refs = refs[n_in + 5:n_in + 5 + n_out]
    wg_ref, wu_ref, wd_ref, wg_stage, wu_stage, wd_stage, sems = refs[n_in + 5 + n_out:]
    step = pl.program_id(0)
    on_prompt = step < n_prompt_steps
    d_ff = wg_ref.shape[1]
    chunks = [slice(c0, c0 + FFN_CHUNK) for c0 in range(0, d_ff, FFN_CHUNK)]

    def weight_copies(c):
        cols, slot = chunks[c], c % STAGE_SLOTS
        return (pltpu.make_async_copy(wg_hbm.at[:, cols], wg_stage.at[slot], sems.at[0, slot]),
                pltpu.make_async_copy(wu_hbm.at[:, cols], wu_stage.at[slot], sems.at[1, slot]),
                pltpu.make_async_copy(wd_hbm.at[cols, :], wd_stage.at[slot], sems.at[2, slot]))

    def start_weight_chunk(c):
        for copy in weight_copies(c):
            copy.start()

    def land_weight_chunk(c):
        cols, slot = chunks[c], c % STAGE_SLOTS
        for copy in weight_copies(c):
            copy.wait()
        wg_ref[:, cols] = wg_stage[slot].astype(BF16)
        wu_ref[:, cols] = wu_stage[slot].astype(BF16)
        wd_ref[cols, :] = wd_stage[slot].astype(BF16)
        if c + STAGE_SLOTS < len(chunks):
            start_weight_chunk(c + STAGE_SLOTS)

    def tile(first_step):
        if split_in:
            xa_ref, xb_ref = x_refs
            x = jnp.where(on_prompt, xa_ref[0], xb_ref[...].reshape(xa_ref.shape[1:]))
        else:
            x = x_refs[0][...]
        h = _rms(x, gain_ref[...]).astype(BF16)
        act = []
        for c, cols in enumerate(chunks):
            if first_step:
                land_weight_chunk(c)
            g = _dot(h, wg_ref[:, cols])
            u = _dot(h, wu_ref[:, cols])
            act.append((g * jax.nn.sigmoid(g) * u).astype(BF16))
        y = x + 0.5 * _dot(jnp.concatenate(act, axis=-1), wd_ref[...])
        if final_norm:
            y = _rms(y, gain_out_ref[...])
        if split_out:
            oa_ref, ob_ref = o_refs

            @pl.when(on_prompt)
            def _():
                oa_ref[0] = y

            @pl.when(jnp.logical_not(on_prompt))
            def _():
                ob_ref[...] = y.reshape(ob_ref.shape)
        else:
            o_refs[0][...] = y

    @pl.when(step == 0)
    def _():
        for c in range(STAGE_SLOTS):
            start_weight_chunk(c)
        tile(first_step=True)

    @pl.when(step > 0)
    def _():
        tile(first_step=False)


def _ffn(x, gain, wg, wu, wd, gain_out, *, group_shapes, split_in, split_out, final_norm, name):
    (b, s, d), (nb, tb, _) = group_shapes
    d_ff = wg.shape[1]
    t = FFN_TOKENS
    assert s % t == 0 and tb == SUBLANES and (nb * tb) % t == 0 and d_ff % FFN_CHUNK == 0
    assert FFN_CHUNK % MXU_COLS == 0
    n_prompt_steps = b * s // t
    n_steps = n_prompt_steps + nb * tb // t
    sample_mode = pl.Buffered(1) if nb * tb == t else None
    pair_specs = _group_specs(n_prompt_steps, s // t, t // tb, t, d, sample_mode)
    flat_spec = pl.BlockSpec((t, d), lambda i: (i, 0))
    pair_shapes = [jax.ShapeDtypeStruct(shape, F32) for shape in group_shapes]
    flat_shape = jax.ShapeDtypeStruct((n_steps * t, d), F32)
    in_hbm = pl.BlockSpec(memory_space=pl.ANY)
    weight_bytes = 3 * d * d_ff * 2 + 3 * STAGE_SLOTS * d * FFN_CHUNK * wg.dtype.itemsize
    assert weight_bytes + 9 * t * d * 4 < VMEM_CLAIM_BYTES < V7X_VMEM_BYTES
    return pl.pallas_call(
        functools.partial(_ffn_kernel, n_prompt_steps=n_prompt_steps, split_in=split_in,
                          split_out=split_out, final_norm=final_norm),
        grid=(n_steps,),
        in_specs=[*(pair_specs if split_in else (flat_spec,)),
                  _resident((1, d)), in_hbm, in_hbm, in_hbm, _resident((1, d))],
        out_specs=list(pair_specs) if split_out else flat_spec,
        out_shape=pair_shapes if split_out else flat_shape,
        scratch_shapes=[pltpu.VMEM((d, d_ff), BF16), pltpu.VMEM((d, d_ff), BF16),
                        pltpu.VMEM((d_ff, d), BF16),
                        pltpu.VMEM((STAGE_SLOTS, d, FFN_CHUNK), wg.dtype),
                        pltpu.VMEM((STAGE_SLOTS, d, FFN_CHUNK), wu.dtype),
                        pltpu.VMEM((STAGE_SLOTS, FFN_CHUNK, d), wd.dtype),
                        pltpu.SemaphoreType.DMA((3, STAGE_SLOTS))],
        compiler_params=pltpu.CompilerParams(
            dimension_semantics=("arbitrary",), vmem_limit_bytes=VMEM_CLAIM_BYTES),
        name=name,
    )(*(x if split_in else (x,)), gain, wg, wu, wd, gain_out)


def _rows(lead, start, size, lanes=slice(None)):
    return lead + (slice(start, start + size), lanes)


def _mix_rows(x, h, win_ref, conv_ext, pool_ext, pool_lvl, first_pos,
              convw_ref, poolw_ref, pscale_ref, wout_ref):
    c = conv_ext.shape[-1]
    t = x.shape[-2]
    lead = (slice(None),) * (x.ndim - 2)

    def project(k):
        return _dot(h, win_ref[:, k * c:(k + 1) * c]).reshape(x.shape[:-1] + (c,))

    p, gc, u, gb = project(3), project(1), project(2), project(0)
    pool_ext[_rows(lead, POOL_PAD, t)] = p
    gdim = c // len(POOL_WINDOWS)
    wsums = []
    if pool_lvl is None:
        for g, w in enumerate(POOL_WINDOWS):
            lanes = slice(g * gdim, (g + 1) * gdim)
            wsum = pool_ext[_rows(lead, POOL_PAD, t, lanes)]
            for j in range(1, w):
                wsum = wsum + pool_ext[_rows(lead, POOL_PAD - j, t, lanes)]
            wsums.append(wsum)
    else:
        span = POOL_PAD - SUBLANES + t
        src = pool_ext
        for g, w in enumerate(POOL_WINDOWS):
            lanes = slice(g * gdim, c)
            shift = w // 2
            if g < len(pool_lvl):
                dst = pool_lvl[g]
                dst[_rows(lead, SUBLANES, span, lanes)] = (
                    src[_rows(lead, SUBLANES, span, lanes)]
                    + src[_rows(lead, SUBLANES - shift, span, lanes)])
                wsums.append(dst[_rows(lead, POOL_PAD, t, slice(g * gdim, (g + 1) * gdim))])
                src = dst
            else:
                wsums.append(src[_rows(lead, POOL_PAD, t, lanes)]
                             + src[_rows(lead, POOL_PAD - shift, t, lanes)])

    pos = first_pos + jax.lax.broadcasted_iota(jnp.int32, (t, 1), 0)
    y_pool = []
    for g, w in enumerate(POOL_WINDOWS):
        lanes = slice(g * gdim, (g + 1) * gdim)
        cnt = jnp.minimum(pos + 1, w).astype(F32)
        dlt = (wsums[g] / cnt - p[..., lanes]).astype(BF16)
        y = _dot(dlt.reshape(-1, gdim), poolw_ref[g]).reshape(dlt.shape)
        y_pool.append(y * pscale_ref[:, lanes])

    conv_ext[_rows(lead, CONV_PAD, t)] = gc * u
    conv = None
    for k in range(CONV_W):
        term = conv_ext[_rows(lead, CONV_PAD - (CONV_W - 1) + k, t)] * convw_ref[k:k + 1, :]
        conv = term if conv is None else conv + term
    y_conv = gb * conv

    mix = jnp.concatenate([y_conv] + y_pool, axis=-1).astype(BF16)
    out = _dot(mix.reshape(-1, 2 * c), wout_ref[...])
    return x + out.reshape(x.shape)


def _state_rows(ext, pad, t, n):
    lead = (slice(None),) * (len(ext.shape) - 2)
    return ext[_rows(lead, pad + t - n, n)]


def _mixer_kernel(x_ref, conv_hist_ref, pool_hist_ref, gain_ref, win_hbm, convw_ref,
                  poolw_ref, pscale_ref, wout_hbm,
                  o_ref, conv_a_ref, pool_a_ref, conv_b_ref, pool_b_ref,
                  conv_ext_a, pool_ext_a, lvl1_a, lvl2_a, lvl3_a, conv_ext_b, pool_ext_b,
                  win_ref, wout_ref, stage, sems,
                  *, n_prompt_steps, steps_per_seq):
    i = pl.program_id(0)
    t, d = x_ref.shape
    weights = (convw_ref, poolw_ref, pscale_ref, wout_ref)

    def land_weights():
        width = stage.shape[2]
        parts = [(hbm, ref, slice(c0, c0 + width))
                 for hbm, ref in ((win_hbm, win_ref), (wout_hbm, wout_ref))
                 for c0 in range(0, ref.shape[1], width)]

        def copy(k):
            hbm, _, cols = parts[k]
            slot = k % STAGE_SLOTS
            return pltpu.make_async_copy(hbm.at[:, cols], stage.at[slot], sems.at[slot])

        for k in range(STAGE_SLOTS):
            copy(k).start()
        for k, (_, ref, cols) in enumerate(parts):
            copy(k).wait()
            ref[:, cols] = stage[k % STAGE_SLOTS].astype(BF16)
            if k + STAGE_SLOTS < len(parts):
                copy(k + STAGE_SLOTS).start()

    @pl.when(i == 0)
    def _():
        land_weights()
        for ref in (pool_ext_a, lvl1_a, lvl2_a):
            ref[0:POOL_PAD - POOL_HIST, :] = jnp.zeros((POOL_PAD - POOL_HIST, ref.shape[-1]), F32)

    @pl.when(i < n_prompt_steps)
    def _():
        ti = i % steps_per_seq

        @pl.when(ti == 0)
        def _():
            conv_ext_a[0:CONV_PAD, :] = jnp.zeros((CONV_PAD, conv_ext_a.shape[1]), F32)
            pool_ext_a[SUBLANES:POOL_PAD, :] = jnp.zeros(
                (POOL_PAD - SUBLANES, pool_ext_a.shape[1]), F32)

        @pl.when(ti > 0)
        def _():
            conv_ext_a[0:CONV_PAD, :] = conv_ext_a[t:t + CONV_PAD, :]
            pool_ext_a[SUBLANES:POOL_PAD, :] = pool_ext_a[t + SUBLANES:t + POOL_PAD, :]

        x = x_ref[...]
        h = _rms(x, gain_ref[...]).astype(BF16)
        o_ref[...] = _mix_rows(x, h, win_ref, conv_ext_a, pool_ext_a, (lvl1_a, lvl2_a, lvl3_a),
                               ti * t, *weights)
        conv_a_ref[0] = _state_rows(conv_ext_a, CONV_PAD, t, CONV_W - 1)
        pool_a_ref[0] = _state_rows(pool_ext_a, POOL_PAD, t, POOL_HIST)

    @pl.when(i >= n_prompt_steps)
    def _():
        bs, tb = conv_ext_b.shape[0], conv_ext_b.shape[1] - CONV_PAD
        conv_ext_b[:, CONV_PAD - (CONV_W - 1):CONV_PAD, :] = conv_hist_ref[...]
        pool_ext_b[:, POOL_PAD - POOL_HIST:POOL_PAD, :] = pool_hist_ref[...]
        x = x_ref[...]
        h = _rms(x, gain_ref[...]).astype(BF16)
        y = _mix_rows(x.reshape(bs, tb, d), h, win_ref, conv_ext_b, pool_ext_b, None, PAST_LEN,
                      *weights)
        o_ref[...] = y.reshape(t, d)
        conv_b_ref[...] = _state_rows(conv_ext_b, CONV_PAD, tb, CONV_W - 1)
        pool_b_ref[...] = _state_rows(pool_ext_b, POOL_PAD, tb, POOL_HIST)


def _mixer(x, conv_hist, pool_hist, gain, win, convw, poolw, pscale, wout, *, group_shapes):
    (b, s, d), (nb, tb, _) = group_shapes
    in_dim = win.shape[1]
    c = in_dim // 4
    groups = len(POOL_WINDOWS)
    t = MIX_TOKENS
    bs = t // tb
    assert s % t == 0 and tb == SUBLANES and nb % bs == 0
    n_prompt_steps = b * s // t
    n_steps = n_prompt_steps + nb // bs
    assert x.shape == (n_steps * t, d)
    flat_spec = pl.BlockSpec((t, d), lambda i: (i, 0))
    in_hbm = pl.BlockSpec(memory_space=pl.ANY)

    def prompt_state(rows):
        return pl.BlockSpec(
            (1, rows, c), lambda i: (jnp.minimum(i, n_prompt_steps - 1) // (s // t), 0, 0))

    def sample_state(rows):
        return pl.BlockSpec((bs, rows, c), lambda i: (jnp.maximum(i - n_prompt_steps, 0), 0, 0))

    def state_shape(n, rows):
        return jax.ShapeDtypeStruct((n, rows, c), F32)

    def ext_scratch(lead, rows):
        return pltpu.VMEM(lead + (rows, c), F32)

    scratch = ([ext_scratch((), CONV_PAD + t)] + [ext_scratch((), POOL_PAD + t)] * 4
               + [ext_scratch((bs,), CONV_PAD + tb), ext_scratch((bs,), POOL_PAD + tb)])
    assert wout.shape == (d, d) and in_dim % c == 0
    scratch += [pltpu.VMEM((d, in_dim), BF16), pltpu.VMEM(wout.shape, BF16),
                pltpu.VMEM((STAGE_SLOTS, d, c), win.dtype), pltpu.SemaphoreType.DMA((STAGE_SLOTS,))]
    weight_bytes = (d * in_dim + 2 * c * d) * 2 + STAGE_SLOTS * d * c * 4 + c * c * 4
    scratch_bytes = (5 * (POOL_PAD + t) + 2 * bs * (POOL_PAD + tb)) * c * 4
    assert weight_bytes + 12 * t * d * 4 + scratch_bytes < VMEM_CLAIM_BYTES < V7X_VMEM_BYTES
    return pl.pallas_call(
        functools.partial(_mixer_kernel, n_prompt_steps=n_prompt_steps, steps_per_seq=s // t),
        grid=(n_steps,),
        in_specs=[flat_spec, sample_state(CONV_W - 1), sample_state(POOL_HIST),
                  _resident((1, d)), in_hbm, _resident((CONV_W, c)),
                  _resident((groups, c // groups, c // groups)), _resident((1, c)), in_hbm],
        out_specs=[flat_spec, prompt_state(CONV_W - 1), prompt_state(POOL_HIST),
                   sample_state(CONV_W - 1), sample_state(POOL_HIST)],
        out_shape=[jax.ShapeDtypeStruct(x.shape, F32),
                   state_shape(b, CONV_W - 1), state_shape(b, POOL_HIST),
                   state_shape(nb, CONV_W - 1), state_shape(nb, POOL_HIST)],
        scratch_shapes=scratch,
        compiler_params=pltpu.CompilerParams(
            dimension_semantics=("arbitrary",), vmem_limit_bytes=VMEM_CLAIM_BYTES),
        name="mixer",
    )(x, conv_hist, pool_hist, gain, win, convw, poolw, pscale, wout)


def kernel(x_prompt, x_sample, state_conv, state_pool, norm_ffn1, ffn1_gate, ffn1_up, ffn1_down,
           norm_mix, w_in, conv_w, pool_w, pool_scale, w_out, norm_ffn2, ffn2_gate, ffn2_up,
           ffn2_down, norm_final):
    row = lambda v: v.reshape(1, -1)
    gain_final = row(norm_final)
    group_shapes = (x_prompt.shape, x_sample.shape)
    x = _ffn((x_prompt, x_sample), row(norm_ffn1), ffn1_gate, ffn1_up, ffn1_down, gain_final,
             group_shapes=group_shapes, split_in=True, split_out=False, final_norm=False,
             name="ffn1")
    x, cp, pp, cs, ps = _mixer(x, state_conv, state_pool, row(norm_mix), w_in, conv_w, pool_w,
                               row(pool_scale), w_out, group_shapes=group_shapes)
    yp, ys = _ffn(x, row(norm_ffn2), ffn2_gate, ffn2_up, ffn2_down, gain_final,
                  group_shapes=group_shapes, split_in=False, split_out=True, final_norm=True,
                  name="ffn2")
    return (yp, ys, cp, pp, cs, ps)
```
